```python
import jax, jax.numpy as jnp
from jax import lax
import numpy as np

D_MODEL = 2048
BATCH = 4
SEQ = 4096
DEPTH = 2

CTX_LEN = 256
GRID_W = 64
N_MOD = 6
EPS = 1e-6

D_CONV = D_MODEL // 2
CONV_WIDTH = 3
D_POOL = D_MODEL // 2
POOL_WINDOWS = (2, 4, 8, 16)
N_POOL_GROUPS = len(POOL_WINDOWS)
POOL_GROUP_DIM = D_POOL // N_POOL_GROUPS
NA_HEADS = 8
NA_HEAD_DIM = 128
D_ATTN = NA_HEADS * NA_HEAD_DIM
WIN_ROWS = 8
WIN_COLS = 16
N_BRANCHES = 3

N_EXPERTS = 16
EXPERT_HIDDEN = 2048
CAPACITY_FACTOR = 2

OFF_CONV = 0
OFF_POOL = OFF_CONV + 3 * D_CONV
OFF_Q = OFF_POOL + D_POOL
OFF_K = OFF_Q + D_ATTN
OFF_V = OFF_K + D_ATTN
OFF_GATE = OFF_V + D_ATTN
IN_COLS = OFF_GATE + N_BRANCHES * D_MODEL

kernel_name = "hybrid_conv_pool_natten_ecmoe_prefix_trunk"


def rms_norm(x, g):
    xf = x.astype(jnp.float32)
    y = xf * lax.rsqrt(jnp.mean(xf * xf, axis=-1, keepdims=True) + EPS)
    return (y * g.astype(jnp.float32)).astype(x.dtype)


def adaln(cond, w_mod, b_mod):
    m = jax.nn.silu(cond) @ w_mod + b_mod
    return m.reshape(*cond.shape[:-1], N_MOD, D_MODEL)


def mod_chunk(m, i):
    return m[..., None, i, :]


def modulate(h, shift, scale):
    return h * (1 + scale) + shift


def short_conv(u, w):
    n = u.shape[1]
    up = jnp.pad(u, ((0, 0), (1, 1), (0, 0)))
    return up[:, 0:n] * w[0] + up[:, 1:n + 1] * w[1] + up[:, 2:n + 2] * w[2]


def multiscale_pool(u, pool_w, pool_scale):
    b, n, _ = u.shape
    ug = u.astype(jnp.float32).reshape(b, n, N_POOL_GROUPS, POOL_GROUP_DIM)
    cs = jnp.concatenate([jnp.zeros_like(ug[:, :1]), jnp.cumsum(ug, axis=1)], axis=1)
    t = jnp.arange(n)[:, None]
    win = jnp.array(POOL_WINDOWS, dtype=jnp.int32)[None, :]
    lo = jnp.clip(t - win // 2, 0, n)
    hi = jnp.clip(t - win // 2 + win, 0, n)
    g = jnp.arange(N_POOL_GROUPS)[None, :]
    count = (hi - lo).astype(jnp.float32)[None, :, :, None]
    mean = (cs[:, hi, g] - cs[:, lo, g]) / count
    mixed = (mean - ug).astype(u.dtype)
    y = jnp.einsum('bngc,gcd->bngd', mixed, pool_w).reshape(b, n, D_POOL)
    return y * pool_scale


def split_heads(proj, off):
    b, n, _ = proj.shape
    return proj[..., off:off + D_ATTN].reshape(b, n, NA_HEADS, NA_HEAD_DIM)


def context_attention(q, k, v):
    s = jnp.einsum('bqhd,bkhd->bhqk', q, k).astype(jnp.float32) * (NA_HEAD_DIM ** -0.5)
    p = jax.nn.softmax(s, axis=-1).astype(v.dtype)
    o = jnp.einsum('bhqk,bkhd->bqhd', p, v)
    return o.reshape(q.shape[0], q.shape[1], D_ATTN)


def neighbourhood_attention(q, k, v, k_ctx, v_ctx, rpb):
    b, n, h, dh = q.shape
    rows = n // GRID_W
    kr = min(WIN_ROWS, rows)
    qg = q.reshape(b, rows, GRID_W, h, dh)
    kg = k.reshape(b, rows, GRID_W, h, dh)
    vg = v.reshape(b, rows, GRID_W, h, dh)
    col = jnp.arange(GRID_W)
    col_start = jnp.clip(col - WIN_COLS // 2, 0, GRID_W - WIN_COLS)
    col_idx = col_start[:, None] + jnp.arange(WIN_COLS)[None, :]
    bias_cols = rpb[:, :, col_idx - col[:, None] + WIN_COLS - 1]
    scale = dh ** -0.5
    n_win = kr * WIN_COLS

    def row_block(r):
        rs = jnp.clip(r - kr // 2, 0, rows - kr)
        q_r = lax.dynamic_index_in_dim(qg, r, axis=1, keepdims=False)
        k_win = lax.dynamic_slice_in_dim(kg, rs, kr, axis=1)[:, :, col_idx]
        v_win = lax.dynamic_slice_in_dim(vg, rs, kr, axis=1)[:, :, col_idx]
        bias = jnp.take(bias_cols, rs + jnp.arange(kr) - r + WIN_ROWS - 1, axis=1)
        s_win = (jnp.einsum('bqhd,brqjhd->bhqrj', q_r, k_win).astype(jnp.float32) * scale
                 + jnp.transpose(bias, (0, 2, 1, 3)).astype(jnp.float32))
        s_ctx = jnp.einsum('bqhd,blhd->bhql', q_r, k_ctx).astype(jnp.float32) * scale
        s = jnp.concatenate([s_win.reshape(b, h, GRID_W, n_win), s_ctx], axis=-1)
        p = jax.nn.softmax(s, axis=-1).astype(v.dtype)
        p_win = p[..., :n_win].reshape(b, h, GRID_W, kr, WIN_COLS)
        p_ctx = p[..., n_win:]
        return (jnp.einsum('bhqrj,brqjhd->bqhd', p_win, v_win)
                + jnp.einsum('bhql,blhd->bqhd', p_ctx, v_ctx))

    out = lax.map(row_block, jnp.arange(rows))
    return jnp.moveaxis(out, 0, 1).reshape(b, n, h * dh)


def merge_branches(proj, attn, conv_w, pool_w, pool_scale, w_conv_out, w_pool_out, w_attn_out, w_o):
    b, n, _ = proj.shape
    a_b = proj[..., OFF_CONV:OFF_CONV + D_CONV]
    a_c = proj[..., OFF_CONV + D_CONV:OFF_CONV + 2 * D_CONV]
    a_x = proj[..., OFF_CONV + 2 * D_CONV:OFF_POOL]
    y_conv = (a_b * short_conv(a_c * a_x, conv_w)) @ w_conv_out
    y_pool = multiscale_pool(proj[..., OFF_POOL:OFF_Q], pool_w, pool_scale) @ w_pool_out
    y_attn = attn @ w_attn_out
    gates = jax.nn.sigmoid(proj[..., OFF_GATE:].reshape(b, n, N_BRANCHES, D_MODEL))
    merged = gates[..., 0, :] * y_conv + gates[..., 1, :] * y_pool + gates[..., 2, :] * y_attn
    return merged @ w_o


def expert_choice_ffn(h, w_router, w_gate, w_up, w_down):
    b, n, _ = h.shape
    cap = max(1, CAPACITY_FACTOR * n // N_EXPERTS)
    aff = jax.nn.softmax((h @ w_router).astype(jnp.float32), axis=-1)
    g, idx = lax.top_k(jnp.swapaxes(aff, 1, 2), cap)
    bidx = jnp.arange(b)[:, None, None]
    xs = h[bidx, idx]
    hid = (jax.nn.silu(jnp.einsum('becd,edf->becf', xs, w_gate))
           * jnp.einsum('becd,edf->becf', xs, w_up))
    ys = jnp.einsum('becf,efd->becd', hid, w_down) * g[..., None].astype(h.dtype)
    return jnp.zeros_like(h).at[bidx, idx].add(ys)


def setup_inputs(seed: int = 0) -> dict:
    key = jax.random.key(seed)
    ks = jax.random.split(key, 24)
    f32 = jnp.float32

    def nrm(k, shape, s):
        return jax.random.normal(k, shape, f32) * s

    return {
        "x": nrm(ks[0], (BATCH, SEQ, D_MODEL), 1.0),
        "c": nrm(ks[1], (BATCH, D_MODEL), 1.0),
        "ctx": nrm(ks[2], (BATCH, CTX_LEN, D_MODEL), 1.0),
        "c_ctx": nrm(ks[3], (D_MODEL,), 1.0),
        "w_mod": nrm(ks[4], (DEPTH, D_MODEL, N_MOD * D_MODEL), 0.5 * D_MODEL ** -0.5),
        "b_mod": nrm(ks[5], (DEPTH, N_MOD * D_MODEL), 0.02),
        "norm1": 1.0 + nrm(ks[6], (DEPTH, D_MODEL), 0.05),
        "norm2": 1.0 + nrm(ks[7], (DEPTH, D_MODEL), 0.05),
        "w_in": nrm(ks[8], (DEPTH, D_MODEL, IN_COLS), D_MODEL ** -0.5),
        "conv_w": nrm(ks[9], (DEPTH, CONV_WIDTH, D_CONV), CONV_WIDTH ** -0.5),
        "pool_w": nrm(ks[10], (DEPTH, N_POOL_GROUPS, POOL_GROUP_DIM, POOL_GROUP_DIM), POOL_GROUP_DIM ** -0.5),
        "pool_scale": 1.0 + nrm(ks[11], (DEPTH, D_POOL), 0.05),
        "rpb": nrm(ks[12], (DEPTH, NA_HEADS, 2 * WIN_ROWS - 1, 2 * WIN_COLS - 1), 0.1),
        "w_conv_out": nrm(ks[13], (DEPTH, D_CONV, D_MODEL), D_CONV ** -0.5),
        "w_pool_out": nrm(ks[14], (DEPTH, D_POOL, D_MODEL), D_POOL ** -0.5),
        "w_attn_out": nrm(ks[15], (DEPTH, D_ATTN, D_MODEL), D_ATTN ** -0.5),
        "w_o": nrm(ks[16], (DEPTH, D_MODEL, D_MODEL), D_MODEL ** -0.5),
        "w_router": nrm(ks[17], (DEPTH, D_MODEL, N_EXPERTS), D_MODEL ** -0.5),
        "w_e_gate": nrm(ks[18], (DEPTH, N_EXPERTS, D_MODEL, EXPERT_HIDDEN), D_MODEL ** -0.5),
        "w_e_up": nrm(ks[19], (DEPTH, N_EXPERTS, D_MODEL, EXPERT_HIDDEN), D_MODEL ** -0.5),
        "w_e_down": nrm(ks[20], (DEPTH, N_EXPERTS, EXPERT_HIDDEN, D_MODEL), EXPERT_HIDDEN ** -0.5),
        "final_norm": 1.0 + nrm(ks[21], (D_MODEL,), 0.05),
    }


def reference(x, c, ctx, c_ctx, w_mod, b_mod, norm1, norm2, w_in, conv_w, pool_w, pool_scale, rpb,
              w_conv_out, w_pool_out, w_attn_out, w_o, w_router, w_e_gate, w_e_up, w_e_down, final_norm):
    for l in range(DEPTH):
        last = l == DEPTH - 1
        mod_x = adaln(c, w_mod[l], b_mod[l])
        mod_c = adaln(c_ctx, w_mod[l], b_mod[l])

        hc = modulate(rms_norm(ctx, norm1[l]), mod_chunk(mod_c, 0), mod_chunk(mod_c, 1))
        if last:
            kv_c = hc @ w_in[l][:, OFF_K:OFF_GATE]
            k_ctx = split_heads(kv_c, 0)
            v_ctx = split_heads(kv_c, D_ATTN)
        else:
            pc = hc @ w_in[l]
            k_ctx = split_heads(pc, OFF_K)
            v_ctx = split_heads(pc, OFF_V)
            attn_c = context_attention(split_heads(pc, OFF_Q), k_ctx, v_ctx)
            ctx_next = ctx + mod_chunk(mod_c, 2) * merge_branches(
                pc, attn_c, conv_w[l], pool_w[l], pool_scale[l],
                w_conv_out[l], w_pool_out[l], w_attn_out[l], w_o[l])
            hc2 = modulate(rms_norm(ctx_next, norm2[l]), mod_chunk(mod_c, 3), mod_chunk(mod_c, 4))
            ctx_next = ctx_next + mod_chunk(mod_c, 5) * expert_choice_ffn(
                hc2, w_router[l], w_e_gate[l], w_e_up[l], w_e_down[l])

        hx = modulate(rms_norm(x, norm1[l]), mod_chunk(mod_x, 0), mod_chunk(mod_x, 1))
        px = hx @ w_in[l]
        attn_x = neighbourhood_attention(split_heads(px, OFF_Q), split_heads(px, OFF_K),
                                         split_heads(px, OFF_V), k_ctx, v_ctx, rpb[l])
        x = x + mod_chunk(mod_x, 2) * merge_branches(
            px, attn_x, conv_w[l], pool_w[l], pool_scale[l],
            w_conv_out[l], w_pool_out[l], w_attn_out[l], w_o[l])
        hx2 = modulate(rms_norm(x, norm2[l]), mod_chunk(mod_x, 3), mod_chunk(mod_x, 4))
        x = x + mod_chunk(mod_x, 5) * expert_choice_ffn(
            hx2, w_router[l], w_e_gate[l], w_e_up[l], w_e_down[l])

        if not last:
            ctx = ctx_next
    return rms_norm(x, final_norm)
```

```python
import functools

import jax
import jax.numpy as jnp
from jax import lax
from jax.experimental import pallas as pl
from jax.experimental.pallas import tpu as pltpu

F32 = jnp.float32
BF16 = jnp.bfloat16
I32 = jnp.int32

GRID_W = 64
POOL_WINDOWS = (2, 4, 8, 16)
CAPACITY_FACTOR = 2
N_MOD = 6
EPS = 1e-6
NEG = -1e30

LANE = 128
SUBLANE = 8
MIB = 1024 * 1024
Q_ROWS = 4
SLOT_BLK = 128
TOK_CHUNK = 256
POOL_PAD = 8


def _cp(sem, vmem_mib=48):
    return pltpu.CompilerParams(dimension_semantics=sem, vmem_limit_bytes=vmem_mib * MIB)


def _dot(a, b):
    return jnp.dot(a, b, preferred_element_type=F32)


def _dot_nt(a, b):
    return lax.dot_general(a, b, (((1,), (1,)), ((), ())), preferred_element_type=F32)


def _dot_tn(a, b):
    return lax.dot_general(a, b, (((0,), (0,)), ((), ())), preferred_element_type=F32)


def _tile(n, pref):
    return pref if n % pref == 0 else n


def _mod_kernel(c_ref, w_ref, b_ref, o_ref):
    c = c_ref[...]
    a = (c * jax.nn.sigmoid(c)).astype(BF16)
    o_ref[0] = _dot(a, w_ref[0].astype(BF16)) + b_ref[0]


def _adaln_all(cond, w_mod, b_mod):
    n_layers, d, cols = w_mod.shape
    r = cond.shape[0]
    tn = _tile(cols, 1024)
    return pl.pallas_call(
        _mod_kernel,
        grid=(n_layers, cols // tn),
        in_specs=[
            pl.BlockSpec((r, d), lambda l, j: (0, 0)),
            pl.BlockSpec((1, d, tn), lambda l, j: (l, 0, j)),
            pl.BlockSpec((1, 1, tn), lambda l, j: (l, 0, j)),
        ],
        out_specs=pl.BlockSpec((1, r, tn), lambda l, j: (l, 0, j)),
        out_shape=jax.ShapeDtypeStruct((n_layers, r, cols), F32),
        compiler_params=_cp(("arbitrary", "arbitrary")),
        name="adaln_mod",
    )(cond, w_mod, b_mod.reshape(n_layers, 1, cols))


def _rms(x, g):
    return x * lax.rsqrt(jnp.mean(x * x, axis=-1, keepdims=True) + EPS) * g


def _norm_mod_kernel(x_ref, g_ref, mod_ref, o_ref, *, shift_i, scale_i):
    m = mod_ref[0]
    y = _rms(x_ref[0], g_ref[...])
    o_ref[0] = (y * (1.0 + m[scale_i:scale_i + 1]) + m[shift_i:shift_i + 1]).astype(o_ref.dtype)


def _mod_index(mod):
    return (lambda b: b) if mod.shape[0] > 1 else (lambda b: 0)


def _norm_mod(x, gain, mod, shift_i, scale_i):
    b, n, d = x.shape
    tm = _tile(n, 512)
    mi = _mod_index(mod)
    return pl.pallas_call(
        functools.partial(_norm_mod_kernel, shift_i=shift_i, scale_i=scale_i),
        grid=(b, n // tm),
        in_specs=[
            pl.BlockSpec((1, tm, d), lambda i, j: (i, j, 0)),
            pl.BlockSpec((1, d), lambda i, j: (0, 0)),
            pl.BlockSpec((1, N_MOD, d), lambda i, j: (mi(i), 0, 0)),
        ],
        out_specs=pl.BlockSpec((1, tm, d), lambda i, j: (i, j, 0)),
        out_shape=jax.ShapeDtypeStruct((b, n, d), BF16),
        compiler_params=_cp(("arbitrary", "arbitrary")),
        name="norm_mod",
    )(x, gain.reshape(1, d), mod)


def _final_norm_kernel(x_ref, g_ref, o_ref):
    o_ref[0] = _rms(x_ref[0], g_ref[...])


def _final_norm(x, gain):
    b, n, d = x.shape
    tm = _tile(n, 512)
    return pl.pallas_call(
        _final_norm_kernel,
        grid=(b, n // tm),
        in_specs=[
            pl.BlockSpec((1, tm, d), lambda i, j: (i, j, 0)),
            pl.BlockSpec((1, d), lambda i, j: (0, 0)),
        ],
        out_specs=pl.BlockSpec((1, tm, d), lambda i, j: (i, j, 0)),
        out_shape=jax.ShapeDtypeStruct((b, n, d), F32),
        compiler_params=_cp(("arbitrary", "arbitrary")),
        name="final_norm",
    )(x, gain.reshape(1, d))


def _proj_kernel(h_ref, w_ref, o_ref):
    o_ref[0] = _dot(h_ref[0], w_ref[...]).astype(o_ref.dtype)


def _proj(h, w, out_dtype):
    b, n, k = h.shape
    c = w.shape[1]
    tm, tn = _tile(n, 1024), _tile(c, 512)
    return pl.pallas_call(
        _proj_kernel,
        grid=(b, n // tm, c // tn),
        in_specs=[
            pl.BlockSpec((1, tm, k), lambda i, j, q: (i, j, 0)),
            pl.BlockSpec((k, tn), lambda i, j, q: (0, q)),
        ],
        out_specs=pl.BlockSpec((1, tm, tn), lambda i, j, q: (i, j, q)),
        out_shape=jax.ShapeDtypeStruct((b, n, c), out_dtype),
        compiler_params=_cp(("arbitrary",) * 3),
        name="in_proj",
    )(h, w)


def _conv_proj_kernel(h_ref, wb_ref, wc_ref, wx_ref, ab_ref, u_ref):
    h = h_ref[0]
    ab_ref[0] = _dot(h, wb_ref[...])
    u_ref[0] = _dot(h, wc_ref[...]) * _dot(h, wx_ref[...])


def _conv_proj(h, w):
    b, n, k = h.shape
    dc = w.shape[1] // 3
    tm, tn = _tile(n, 1024), _tile(dc, 256)
    nb = dc // tn
    out = jax.ShapeDtypeStruct((b, n, dc), F32)
    return pl.pallas_call(
        _conv_proj_kernel,
        grid=(b, n // tm, nb),
        in_specs=[
            pl.BlockSpec((1, tm, k), lambda i, j, q: (i, j, 0)),
            pl.BlockSpec((k, tn), lambda i, j, q: (0, q)),
            pl.BlockSpec((k, tn), lambda i, j, q: (0, q + nb)),
            pl.BlockSpec((k, tn), lambda i, j, q: (0, q + 2 * nb)),
        ],
        out_specs=[pl.BlockSpec((1, tm, tn), lambda i, j, q: (i, j, q))] * 2,
        out_shape=[out, out],
        compiler_params=_cp(("arbitrary",) * 3),
        name="conv_proj",
    )(h, w, w, w)


def _conv_kernel(ab_ref, u_ref, w_ref, o_ref):
    u = u_ref[0]
    n = u.shape[0]
    row = lax.broadcasted_iota(I32, u.shape, 0)
    prev = jnp.where(row == 0, 0.0, pltpu.roll(u, 1, 0))
    nxt = jnp.where(row == n - 1, 0.0, pltpu.roll(u, n - 1, 0))
    w = w_ref[...]
    o_ref[0] = (ab_ref[0] * (prev * w[0:1] + u * w[1:2] + nxt * w[2:3])).astype(BF16)


def _short_conv(ab, u, conv_w):
    b, n, dc = u.shape
    tc = _tile(dc, LANE)
    blk = pl.BlockSpec((1, n, tc), lambda i, j: (i, 0, j))
    return pl.pallas_call(
        _conv_kernel,
        grid=(b, dc // tc),
        in_specs=[blk, blk, pl.BlockSpec((conv_w.shape[0], tc), lambda i, j: (0, j))],
        out_specs=blk,
        out_shape=jax.ShapeDtypeStruct((b, n, dc), BF16),
        compiler_params=_cp(("arbitrary", "arbitrary")),
        name="short_conv",
    )(ab, u, conv_w)


def _pool_kernel(u_ref, w_ref, s_ref, o_ref):
    grp = pl.program_id(1)
    u = u_ref[0]
    n, c = u.shape
    npad = n + 2 * POOL_PAD
    pad = jnp.zeros((POOL_PAD, c), F32)
    up = jnp.concatenate([pad, u, pad], axis=0)
    t = lax.broadcasted_iota(I32, (n, c), 0)

    def shifted(x, s):
        return pltpu.roll(x, s % npad, 0)

    for gi, win in enumerate(POOL_WINDOWS):
        @pl.when(grp == gi)
        def _(win=win):
            w = shifted(up, 1) + up
            span = 2
            while span < win:
                half = span // 2
                w = shifted(w, half) + shifted(w, -half)
                span *= 2
            ws = w[POOL_PAD:POOL_PAD + n]
            cnt = jnp.minimum(t + win // 2, n) - jnp.maximum(t - win // 2, 0)
            mixed = ws / cnt.astype(F32) - u
            y = _dot(mixed.astype(BF16), w_ref[0]) * s_ref[0]
            o_ref[0] = y.astype(BF16)


def _multiscale_pool(u, pool_w, pool_scale):
    b, n, dp = u.shape
    g, pg, _ = pool_w.shape
    assert g == len(POOL_WINDOWS) and g * pg == dp
    return pl.pallas_call(
        _pool_kernel,
        grid=(b, g),
        in_specs=[
            pl.BlockSpec((1, n, pg), lambda i, j: (i, 0, j)),
            pl.BlockSpec((1, pg, pg), lambda i, j: (j, 0, 0)),
            pl.BlockSpec((1, 1, pg), lambda i, j: (j, 0, 0)),
        ],
        out_specs=pl.BlockSpec((1, n, pg), lambda i, j: (i, 0, j)),
        out_shape=jax.ShapeDtypeStruct((b, n, dp), BF16),
        compiler_params=_cp(("arbitrary", "arbitrary")),
        name="multiscale_pool",
    )(u, pool_w.astype(BF16), pool_scale.reshape(g, 1, pg))


def _group_geometry(rows, win_rows):
    key_rows = Q_ROWS + win_rows
    n_groups = rows // Q_ROWS
    assert rows % Q_ROWS == 0 and n_groups >= 3 and rows >= key_rows
    reps = (0, 1, n_groups - 1)
    return key_rows, n_groups, reps


def _bias_kernel(rpb_ref, o_ref, *, rows, win_rows, win_cols, n_heads):
    layer, head = pl.program_id(0), pl.program_id(1)
    key_rows, n_groups, reps = _group_geometry(rows, win_rows)
    n_dr, n_dc = 2 * win_rows - 1, 2 * win_cols - 1
    base = (layer * n_heads + head) * n_dr * n_dc
    w = GRID_W
    qc = lax.broadcasted_iota(I32, (w, 2 * w), 0)
    lane = lax.broadcasted_iota(I32, (w, 2 * w), 1)
    kc = jnp.where(lane < w, lane, lane - w)
    first = lane < w
    cstart = jnp.clip(qc - win_cols // 2, 0, w - win_cols)
    in_cols = (kc >= cstart) & (kc < cstart + win_cols)
    dc = kc - qc + win_cols - 1

    def pair_block(dr_a, dr_b):
        acc = jnp.full((w, 2 * w), NEG, F32)
        if dr_a is None and dr_b is None:
            return acc
        for d in range(n_dc):
            va = rpb_ref[base + dr_a * n_dc + d] if dr_a is not None else NEG
            vb = rpb_ref[base + dr_b * n_dc + d] if dr_b is not None else NEG
            acc = jnp.where(in_cols & (dc == d), jnp.where(first, va, vb), acc)
        return acc

    for ti, g in enumerate(reps):
        kb = min(max(Q_ROWS * g - win_rows // 2, 0), rows - key_rows)
        for i in range(Q_ROWS):
            r = Q_ROWS * g + i
            rs = min(max(r - win_rows // 2, 0), rows - win_rows)
            drs = []
            for j in range(key_rows):
                kr = kb + j
                drs.append(kr - r + win_rows - 1 if rs <= kr < rs + win_rows else None)
            for jp in range(key_rows // 2):
                o_ref[0, 0, ti, i * w:(i + 1) * w, jp * 2 * w:(jp + 1) * 2 * w] = pair_block(
                    drs[2 * jp], drs[2 * jp + 1])


def _bias_tiles(rpb, rows):
    n_layers, n_heads, n_dr, n_dc = rpb.shape
    win_rows, win_cols = (n_dr + 1) // 2, (n_dc + 1) // 2
    key_rows, _, _ = _group_geometry(rows, win_rows)
    assert key_rows % 2 == 0
    tq, tk = Q_ROWS * GRID_W, key_rows * GRID_W
    return pl.pallas_call(
        functools.partial(_bias_kernel, rows=rows, win_rows=win_rows, win_cols=win_cols, n_heads=n_heads),
        grid=(n_layers, n_heads),
        in_specs=[pl.BlockSpec(memory_space=pltpu.SMEM)],
        out_specs=pl.BlockSpec((1, 1, 3, tq, tk), lambda l, h: (l, h, 0, 0, 0)),
        out_shape=jax.ShapeDtypeStruct((n_layers, n_heads, 3, tq, tk), F32),
        compiler_params=_cp(("arbitrary", "arbitrary")),
        name="bias_tiles",
    )(rpb.reshape(-1))


def _natten_kernel(q_ref, k_ref, v_ref, kc_ref, vc_ref, bias_ref, o_ref, *, rows, win_rows):
    key_rows, n_groups, _ = _group_geometry(rows, win_rows)
    tq, tk = Q_ROWS * GRID_W, key_rows * GRID_W
    scale = q_ref.shape[-1] ** -0.5
    kc = kc_ref[0]
    vc = vc_ref[0]

    def group(g, carry):
        kb = jnp.clip(Q_ROWS * g - win_rows // 2, 0, rows - key_rows)
        kind = jnp.where(g == 0, 0, jnp.where(g == n_groups - 1, 2, 1))
        q0 = pl.multiple_of(g * tq, tq)
        k0 = pl.multiple_of(kb * GRID_W, GRID_W)
        q = q_ref[0, pl.ds(q0, tq), :]
        kw = k_ref[0, pl.ds(k0, tk), :]
        vw = v_ref[0, pl.ds(k0, tk), :]
        s_win = _dot_nt(q, kw) * scale + bias_ref[0, 0, kind]
        s_ctx = _dot_nt(q, kc) * scale
        m = jnp.maximum(jnp.max(s_win, axis=-1, keepdims=True), jnp.max(s_ctx, axis=-1, keepdims=True))
        p_win = jnp.exp(s_win - m)
        p_ctx = jnp.exp(s_ctx - m)
        inv = 1.0 / (jnp.sum(p_win, axis=-1, keepdims=True) + jnp.sum(p_ctx, axis=-1, keepdims=True))
        o = _dot((p_win * inv).astype(BF16), vw) + _dot((p_ctx * inv).astype(BF16), vc)
        o_ref[0, pl.ds(q0, tq), :] = o.astype(BF16)
        return carry

    lax.fori_loop(0, n_groups, group, 0)


def _natten(qkv, kv_ctx, k_off, v_off, bias, n_heads):
    b, n, c3 = qkv.shape
    dh = c3 // (3 * n_heads)
    lc = kv_ctx.shape[1]
    rows = n // GRID_W
    win_rows = bias.shape[-1] // GRID_W - Q_ROWS
    seq = lambda off: pl.BlockSpec((1, n, dh), lambda i, h: (i, 0, off + h))
    ctx = lambda off: pl.BlockSpec((1, lc, dh), lambda i, h: (i, 0, off + h))
    return pl.pallas_call(
        functools.partial(_natten_kernel, rows=rows, win_rows=win_rows),
        grid=(b, n_heads),
        in_specs=[
            seq(0), seq(n_heads), seq(2 * n_heads), ctx(k_off), ctx(v_off),
            pl.BlockSpec((1, 1) + bias.shape[2:], lambda i, h: (0, h, 0, 0, 0)),
        ],
        out_specs=pl.BlockSpec((1, n, dh), lambda i, h: (i, 0, h)),
        out_shape=jax.ShapeDtypeStruct((b, n, n_heads * dh), BF16),
        compiler_params=_cp(("arbitrary", "arbitrary")),
        name="natten",
    )(qkv, qkv, qkv, kv_ctx, kv_ctx, bias)


def _ctx_attn_kernel(q_ref, k_ref, v_ref, o_ref):
    q = q_ref[0]
    s = _dot_nt(q, k_ref[0]) * (q.shape[-1] ** -0.5)
    p = jnp.exp(s - jnp.max(s, axis=-1, keepdims=True))
    p = p * (1.0 / jnp.sum(p, axis=-1, keepdims=True))
    o_ref[0] = _dot(p.astype(BF16), v_ref[0]).astype(BF16)


def _ctx_attention(qkv, n_heads):
    b, lc, c3 = qkv.shape
    dh = c3 // (3 * n_heads)
    blk = lambda off: pl.BlockSpec((1, lc, dh), lambda i, h: (i, 0, off + h))
    return pl.pallas_call(
        _ctx_attn_kernel,
        grid=(b, n_heads),
        in_specs=[blk(0), blk(n_heads), blk(2 * n_heads)],
        out_specs=blk(0),
        out_shape=jax.ShapeDtypeStruct((b, lc, n_heads * dh), BF16),
        compiler_params=_cp(("arbitrary", "arbitrary")),
        name="ctx_attention",
    )(qkv, qkv, qkv)


def _merge_kernel(h_ref, zc_ref, zp_ref, za_ref, g0_ref, g1_ref, g2_ref, wc_ref, wp_ref, wa_ref, o_ref):
    h = h_ref[0]
    m = jax.nn.sigmoid(_dot(h, g0_ref[...])) * _dot(zc_ref[0], wc_ref[...])
    m = m + jax.nn.sigmoid(_dot(h, g1_ref[...])) * _dot(zp_ref[0], wp_ref[...])
    m = m + jax.nn.sigmoid(_dot(h, g2_ref[...])) * _dot(za_ref[0], wa_ref[...])
    o_ref[0] = m.astype(BF16)


def _merge(h, zc, zp, za, w_gate, wc, wp, wa):
    b, n, d = h.shape
    tm, tn = _tile(n, 512), _tile(d, 512)
    nb = d // tn
    act = lambda a: pl.BlockSpec((1, tm, a.shape[-1]), lambda i, j, q: (i, j, 0))
    wcol = lambda a, off: pl.BlockSpec((a.shape[0], tn), lambda i, j, q: (0, q + off))
    return pl.pallas_call(
        _merge_kernel,
        grid=(b, n // tm, nb),
        in_specs=[act(h), act(zc), act(zp), act(za),
                  wcol(w_gate, 0), wcol(w_gate, nb), wcol(w_gate, 2 * nb),
                  wcol(wc, 0), wcol(wp, 0), wcol(wa, 0)],
        out_specs=pl.BlockSpec((1, tm, tn), lambda i, j, q: (i, j, q)),
        out_shape=jax.ShapeDtypeStruct((b, n, d), BF16),
        compiler_params=_cp(("arbitrary",) * 3, 56),
        name="merge",
    )(h, zc, zp, za, w_gate, w_gate, w_gate, wc, wp, wa)


def _out_proj_kernel(m_ref, w_ref, x_ref, mod_ref, g_ref, x1_ref, h2_ref):
    mod = mod_ref[0]
    x1 = x_ref[0] + mod[2:3] * _dot(m_ref[0], w_ref[...])
    x1_ref[0] = x1
    h2_ref[0] = (_rms(x1, g_ref[...]) * (1.0 + mod[4:5]) + mod[3:4]).astype(BF16)


def _out_proj(merged, w_o, x, mod, gain2):
    b, n, d = x.shape
    tm = _tile(n, 256)
    mi = _mod_index(mod)
    row = pl.BlockSpec((1, tm, d), lambda i, j: (i, j, 0))
    return pl.pallas_call(
        _out_proj_kernel,
        grid=(b, n // tm),
        in_specs=[row, pl.BlockSpec((d, d), lambda i, j: (0, 0)), row,
                  pl.BlockSpec((1, N_MOD, d), lambda i, j: (mi(i), 0, 0)),
                  pl.BlockSpec((1, d), lambda i, j: (0, 0))],
        out_specs=[row, row],
        out_shape=[jax.ShapeDtypeStruct((b, n, d), F32), jax.ShapeDtypeStruct((b, n, d), BF16)],
        compiler_params=_cp(("arbitrary", "arbitrary")),
        name="out_proj",
    )(merged, w_o, x, mod, gain2.reshape(1, d))


def _router_kernel(h_ref, w_ref, o_ref):
    logits = _dot_nt(w_ref[...], h_ref[0])
    e = jnp.exp(logits - jnp.max(logits, axis=0, keepdims=True))
    o_ref[0] = e / jnp.sum(e, axis=0, keepdims=True)


def _router(h2, w_router_t):
    b, n, d = h2.shape
    e = w_router_t.shape[0]
    tn = _tile(n, 1024)
    return pl.pallas_call(
        _router_kernel,
        grid=(b, n // tn),
        in_specs=[pl.BlockSpec((1, tn, d), lambda i, j: (i, j, 0)),
                  pl.BlockSpec((e, d), lambda i, j: (0, 0))],
        out_specs=pl.BlockSpec((1, e, tn), lambda i, j: (i, 0, j)),
        out_shape=jax.ShapeDtypeStruct((b, e, n), F32),
        compiler_params=_cp(("arbitrary", "arbitrary")),
        name="router",
    )(h2, w_router_t)


def _lane_cumsum(x, chunk):
    e, n = x.shape
    nk = n // chunk
    i = lax.broadcasted_iota(I32, (chunk, chunk), 0)
    j = lax.broadcasted_iota(I32, (chunk, chunk), 1)
    upper = jnp.where(i <= j, 1.0, 0.0).astype(BF16)
    lane = lax.broadcasted_iota(I32, (e, LANE), 1)
    off = jnp.zeros((e, 1), F32)
    bounds = jnp.zeros((e, LANE), F32)
    parts = []
    for k in range(nk):
        bounds = jnp.where(lane == k, off, bounds)
        c = _dot(x[:, k * chunk:(k + 1) * chunk].astype(BF16), upper) + off
        parts.append(c)
        off = c[:, chunk - 1:chunk]
    bounds = jnp.where(lane == nk, off, bounds)
    return jnp.concatenate(parts, axis=1), bounds


def _route_kernel(aff_ref, pos_ref, cum_ref, *, cap, chunk):
    aff = aff_ref[0]
    bits = pltpu.bitcast(aff, I32)

    def search(i, prefix):
        cand = prefix | jnp.left_shift(jnp.int32(1), 30 - i)
        cnt = jnp.sum(jnp.where(bits >= cand, 1.0, 0.0), axis=1, keepdims=True)
        return jnp.where(cnt >= cap, cand, prefix)

    thr = lax.fori_loop(0, 31, search, jnp.zeros((aff.shape[0], 1), I32))
    gt = bits > thr
    eq = bits == thr
    need = cap - jnp.sum(jnp.where(gt, 1.0, 0.0), axis=1, keepdims=True)
    eq_f = jnp.where(eq, 1.0, 0.0)
    eq_rank, _ = _lane_cumsum(eq_f, chunk)
    sel = gt | (eq & (eq_rank - eq_f < need))
    sel_f = jnp.where(sel, 1.0, 0.0)
    rank, bounds = _lane_cumsum(sel_f, chunk)
    pos_ref[0] = jnp.where(sel, (rank - sel_f).astype(I32), -1)
    cum_ref[0] = bounds.astype(I32)


def _route(aff, cap, chunk):
    b, e, n = aff.shape
    assert n // chunk < LANE
    return pl.pallas_call(
        functools.partial(_route_kernel, cap=cap, chunk=chunk),
        grid=(b,),
        in_specs=[pl.BlockSpec((1, e, n), lambda i: (i, 0, 0))],
        out_specs=[pl.BlockSpec((1, e, n), lambda i: (i, 0, 0)),
                   pl.BlockSpec((1, e, LANE), lambda i: (i, 0, 0))],
        out_shape=[jax.ShapeDtypeStruct((b, e, n), I32), jax.ShapeDtypeStruct((b, e, LANE), I32)],
        compiler_params=_cp(("arbitrary",)),
        name="route",
    )(aff)


def _gather_kernel(cum_ref, h_ref, pos_ref, aff_ref, xs_ref, gs_ref, acc_ref, gacc_ref, *, chunk, n_exp):
    b, e = pl.program_id(0), pl.program_id(1)
    nk = pos_ref.shape[2]
    capp = acc_ref.shape[0]
    base = (b * n_exp + e) * LANE
    acc_ref[...] = jnp.zeros_like(acc_ref)
    gacc_ref[...] = jnp.zeros_like(gacc_ref)
    for sb in range(capp // SLOT_BLK):
        s0 = sb * SLOT_BLK
        slot = lax.broadcasted_iota(I32, (SLOT_BLK, chunk), 0) + s0

        def chunk_step(k, carry):
            lo, hi = cum_ref[base + k], cum_ref[base + k + 1]

            @pl.when((lo < s0 + SLOT_BLK) & (hi > s0))
            def _():
                hit = pos_ref[0, e, pl.ds(k, 1), :] == slot
                t0 = pl.multiple_of(k * chunk, chunk)
                acc_ref[s0:s0 + SLOT_BLK, :] += _dot(
                    jnp.where(hit, 1.0, 0.0).astype(BF16), h_ref[0, pl.ds(t0, chunk), :])
                gacc_ref[s0:s0 + SLOT_BLK, :] += jnp.sum(
                    jnp.where(hit, aff_ref[0, e, pl.ds(k, 1), :], 0.0), axis=1, keepdims=True)

            return carry

        lax.fori_loop(0, nk, chunk_step, 0)
    xs_ref[0] = acc_ref[...].astype(BF16)
    gs_ref[0] = gacc_ref[...]


def _gather(h2, pos, aff, cum, capp, chunk):
    b, n, d = h2.shape
    e = pos.shape[1]
    nk = n // chunk
    pos4 = pos.reshape(b, e, nk, chunk)
    aff4 = aff.reshape(b, e, nk, chunk)
    grid_spec = pltpu.PrefetchScalarGridSpec(
        num_scalar_prefetch=1,
        grid=(b, e),
        in_specs=[
            pl.BlockSpec((1, n, d), lambda i, j, c: (i, 0, 0)),
            pl.BlockSpec((1, e, nk, chunk), lambda i, j, c: (i, 0, 0, 0)),
            pl.BlockSpec((1, e, nk, chunk), lambda i, j, c: (i, 0, 0, 0)),
        ],
        out_specs=[pl.BlockSpec((1, capp, d), lambda i, j, c: (j, i, 0)),
                   pl.BlockSpec((1, capp, 1), lambda i, j, c: (j, i, 0))],
        scratch_shapes=[pltpu.VMEM((capp, d), F32), pltpu.VMEM((capp, 1), F32)],
    )
    return pl.pallas_call(
        functools.partial(_gather_kernel, chunk=chunk, n_exp=e),
        grid_spec=grid_spec,
        out_shape=[jax.ShapeDtypeStruct((e, b * capp, d), BF16),
                   jax.ShapeDtypeStruct((e, b * capp, 1), F32)],
        compiler_params=_cp(("arbitrary", "arbitrary"), 56),
        name="moe_gather",
    )(cum.reshape(-1), h2, pos4, aff4)


def _expert_up_kernel(x_ref, wg_ref, wu_ref, o_ref):
    x = x_ref[0]
    g = _dot(x, wg_ref[0].astype(BF16))
    u = _dot(x, wu_ref[0].astype(BF16))
    o_ref[0] = (g * jax.nn.sigmoid(g) * u).astype(BF16)


def _expert_up(xs, w_gate, w_up):
    e, r, d = xs.shape
    f = w_gate.shape[-1]
    tf = _tile(f, 256)
    wblk = pl.BlockSpec((1, d, tf), lambda i, j: (i, 0, j))
    return pl.pallas_call(
        _expert_up_kernel,
        grid=(e, f // tf),
        in_specs=[pl.BlockSpec((1, r, d), lambda i, j: (i, 0, 0)), wblk, wblk],
        out_specs=pl.BlockSpec((1, r, tf), lambda i, j: (i, 0, j)),
        out_shape=jax.ShapeDtypeStruct((e, r, f), BF16),
        compiler_params=_cp(("arbitrary", "arbitrary"), 56),
        name="expert_up",
    )(xs, w_gate, w_up)


def _expert_down_kernel(h_ref, w_ref, g_ref, o_ref):
    o_ref[0] = (_dot(h_ref[0], w_ref[0].astype(BF16)) * g_ref[0]).astype(BF16)


def _expert_down(hid, w_down, gs):
    e, r, f = hid.shape
    d = w_down.shape[-1]
    tn = _tile(d, 256)
    return pl.pallas_call(
        _expert_down_kernel,
        grid=(e, d // tn),
        in_specs=[pl.BlockSpec((1, r, f), lambda i, j: (i, 0, 0)),
                  pl.BlockSpec((1, f, tn), lambda i, j: (i, 0, j)),
                  pl.BlockSpec((1, r, 1), lambda i, j: (i, 0, 0))],
        out_specs=pl.BlockSpec((1, r, tn), lambda i, j: (i, 0, j)),
        out_shape=jax.ShapeDtypeStruct((e, r, d), BF16),
        compiler_params=_cp(("arbitrary", "arbitrary"), 56),
        name="expert_down",
    )(hid, w_down, gs)


def _combine_kernel(cum_ref, pos_ref, ys_ref, x_ref, mod_ref, o_ref, acc_ref, *, n_exp):
    b, t, e = pl.program_id(0), pl.program_id(1), pl.program_id(2)
    chunk = pos_ref.shape[3]
    capp = ys_ref.shape[1]
    base = (b * n_exp + e) * LANE

    @pl.when(e == 0)
    def _():
        acc_ref[...] = jnp.zeros_like(acc_ref)

    lo, hi = cum_ref[base + t], cum_ref[base + t + 1]
    for sb in range(capp // SLOT_BLK):
        s0 = sb * SLOT_BLK

        @pl.when((lo < s0 + SLOT_BLK) & (hi > s0))
        def _(s0=s0):
            slot = lax.broadcasted_iota(I32, (SLOT_BLK, chunk), 0) + s0
            hit = pos_ref[0, e, pl.ds(t, 1), :] == slot
            acc_ref[...] += _dot_tn(jnp.where(hit, 1.0, 0.0).astype(BF16), ys_ref[0, s0:s0 + SLOT_BLK, :])

    @pl.when(e == n_exp - 1)
    def _():
        o_ref[0] = x_ref[0] + mod_ref[0][5:6] * acc_ref[...]


def _combine(ys, pos, cum, x1, mod, capp, chunk):
    b, n, d = x1.shape
    e = pos.shape[1]
    nk = n // chunk
    mi = _mod_index(mod)
    grid_spec = pltpu.PrefetchScalarGridSpec(
        num_scalar_prefetch=1,
        grid=(b, nk, e),
        in_specs=[
            pl.BlockSpec((1, e, nk, chunk), lambda i, t, j, c: (i, 0, 0, 0)),
            pl.BlockSpec((1, capp, d), lambda i, t, j, c: (j, i, 0)),
            pl.BlockSpec((1, chunk, d), lambda i, t, j, c: (i, t, 0)),
            pl.BlockSpec((1, N_MOD, d), lambda i, t, j, c: (mi(i), 0, 0)),
        ],
        out_specs=pl.BlockSpec((1, chunk, d), lambda i, t, j, c: (i, t, 0)),
        scratch_shapes=[pltpu.VMEM((chunk, d), F32)],
    )
    return pl.pallas_call(
        functools.partial(_combine_kernel, n_exp=e),
        grid_spec=grid_spec,
        out_shape=jax.ShapeDtypeStruct((b, n, d), F32),
        compiler_params=_cp(("arbitrary",) * 3),
        name="moe_combine",
    )(cum.reshape(-1), pos.reshape(b, e, nk, chunk), ys, x1, mod)


def _expert_choice_ffn(h2, x1, mod, w_router_t, w_gate, w_up, w_down):
    b, n, d = h2.shape
    e = w_router_t.shape[0]
    cap = max(1, CAPACITY_FACTOR * n // e)
    capp = -(-cap // SLOT_BLK) * SLOT_BLK
    chunk = min(TOK_CHUNK, n)
    aff = _router(h2, w_router_t)
    pos, cum = _route(aff, cap, chunk)
    xs, gs = _gather(h2, pos, aff, cum, capp, chunk)
    hid = _expert_up(xs, w_gate, w_up)
    ys = _expert_down(hid, w_down, gs)
    return _combine(ys, pos, cum, x1, mod, capp, chunk)


def kernel(x, c, ctx, c_ctx, w_mod, b_mod, norm1, norm2, w_in, conv_w, pool_w, pool_scale, rpb,
           w_conv_out, w_pool_out, w_attn_out, w_o, w_router, w_e_gate, w_e_up, w_e_down, final_norm):
    depth = w_mod.shape[0]
    b, n, d = x.shape
    dc, dp = conv_w.shape[-1], pool_scale.shape[-1]
    n_heads, da = rpb.shape[1], w_attn_out.shape[1]
    off_pool, off_q = 3 * dc, 3 * dc + dp
    off_k, off_gate = off_q + da, off_q + 3 * da
    assert n % GRID_W == 0 and da % n_heads == 0

    n_rows = -(-(b + 1) // SUBLANE) * SUBLANE
    cond = jnp.zeros((n_rows, d), F32).at[:b].set(c).at[b].set(c_ctx)
    mod_all = _adaln_all(cond, w_mod, b_mod).reshape(depth, n_rows, N_MOD, d)
    bias_all = _bias_tiles(rpb, n // GRID_W)

    for l in range(depth):
        last = l == depth - 1
        mod_x, mod_c = mod_all[l, :b], mod_all[l, b:b + 1]
        w_l = w_in[l].astype(BF16)
        w_conv_in, w_pool_in = w_l[:, :off_pool], w_l[:, off_pool:off_q]
        w_qkv, w_gates = w_l[:, off_q:off_gate], w_l[:, off_gate:]
        wc, wp, wa = (w_conv_out[l].astype(BF16), w_pool_out[l].astype(BF16), w_attn_out[l].astype(BF16))
        wo = w_o[l].astype(BF16)
        wr_t = w_router[l].T.astype(BF16)

        def mixer(h, attn):
            ab, u = _conv_proj(h, w_conv_in)
            zc = _short_conv(ab, u, conv_w[l])
            zp = _multiscale_pool(_proj(h, w_pool_in, F32), pool_w[l], pool_scale[l])
            return _merge(h, zc, zp, attn, w_gates, wc, wp, wa)

        hc = _norm_mod(ctx, norm1[l], mod_c, 0, 1)
        if last:
            kv_c = _proj(hc, w_l[:, off_k:off_gate], BF16)
            k_off, v_off = 0, n_heads
        else:
            kv_c = _proj(hc, w_qkv, BF16)
            k_off, v_off = n_heads, 2 * n_heads
            merged_c = mixer(hc, _ctx_attention(kv_c, n_heads))
            ctx1, hc2 = _out_proj(merged_c, wo, ctx, mod_c, norm2[l])
            ctx_next = _expert_choice_ffn(hc2, ctx1, mod_c, wr_t, w_e_gate[l], w_e_up[l], w_e_down[l])

        hx = _norm_mod(x, norm1[l], mod_x, 0, 1)
        qkv = _proj(hx, w_qkv, BF16)
        attn = _natten(qkv, kv_c, k_off, v_off, bias_all[l:l + 1], n_heads)
        x1, hx2 = _out_proj(mixer(hx, attn), wo, x, mod_x, norm2[l])
        x = _expert_choice_ffn(hx2, x1, mod_x, wr_t, w_e_gate[l], w_e_up[l], w_e_down[l])
        if not last:
            ctx = ctx_next
    return _final_norm(x, final_norm)
```

```python
import functools

import jax
import jax.numpy as jnp
from jax import lax
from jax.experimental import pallas as pl
from jax.experimental.pallas import tpu as pltpu

F32 = jnp.float32
BF16 = jnp.bfloat16
I32 = jnp.int32

GRID_W = 64
POOL_WINDOWS = (2, 4, 8, 16)
CAPACITY_FACTOR = 2
N_MOD = 6
EPS = 1e-6
NEG = -1e30

LANE = 128
SUBLANE = 8
BF16_ROWS = 16
MIB = 1024 * 1024
Q_ROWS = 4
SLOT_BLK = 128
GATHER_CHUNKS = 12
COMBINE_TOKENS = 256
COMBINE_WIN = 64
POOL_PAD = 8


def _cp(sem, vmem_mib=48):
    return pltpu.CompilerParams(dimension_semantics=sem, vmem_limit_bytes=vmem_mib * MIB)


def _dot(a, b):
    return jnp.dot(a, b, preferred_element_type=F32)


def _dot_nt(a, b):
    return lax.dot_general(a, b, (((1,), (1,)), ((), ())), preferred_element_type=F32)


def _dot_tn(a, b):
    return lax.dot_general(a, b, (((0,), (0,)), ((), ())), preferred_element_type=F32)


def _tile(n, pref):
    return pref if n % pref == 0 else n


def _onehot(hit):
    return jnp.where(hit, 1.0, 0.0).astype(BF16)


def _mod_kernel(c_ref, w_ref, b_ref, o_ref):
    c = c_ref[...]
    a = (c * jax.nn.sigmoid(c)).astype(BF16)
    o_ref[0] = _dot(a, w_ref[0].astype(BF16)) + b_ref[0]


def _adaln_all(cond, w_mod, b_mod):
    n_layers, d, cols = w_mod.shape
    r = cond.shape[0]
    tn = _tile(cols, 1024)
    return pl.pallas_call(
        _mod_kernel,
        grid=(n_layers, cols // tn),
        in_specs=[
            pl.BlockSpec((r, d), lambda l, j: (0, 0)),
            pl.BlockSpec((1, d, tn), lambda l, j: (l, 0, j)),
            pl.BlockSpec((1, 1, tn), lambda l, j: (l, 0, j)),
        ],
        out_specs=pl.BlockSpec((1, r, tn), lambda l, j: (l, 0, j)),
        out_shape=jax.ShapeDtypeStruct((n_layers, r, cols), F32),
        compiler_params=_cp(("arbitrary", "arbitrary")),
        name="adaln_mod",
    )(cond, w_mod, b_mod.reshape(n_layers, 1, cols))


def _mod_spec(mod_all, layer, row_of):
    return pl.BlockSpec((1, 1) + mod_all.shape[2:], lambda i, *_: (layer, row_of(i), 0, 0))


def _gain_spec(gain_all, layer):
    return pl.BlockSpec((1, 1, gain_all.shape[-1]), lambda *_: (layer, 0, 0))


def _rms(x, g):
    return x * lax.rsqrt(jnp.mean(x * x, axis=-1, keepdims=True) + EPS) * g


def _modulated_norm(x, gain, mod, shift_i, scale_i):
    return _rms(x, gain) * (1.0 + mod[scale_i:scale_i + 1]) + mod[shift_i:shift_i + 1]


def _norm_mod_kernel(x_ref, g_ref, mod_ref, o_ref):
    o_ref[0] = _modulated_norm(x_ref[0], g_ref[0], mod_ref[0, 0], 0, 1).astype(BF16)


def _norm_mod(x, gain_all, mod_all, layer, row_of):
    b, n, d = x.shape
    tm = _tile(n, 512)
    return pl.pallas_call(
        _norm_mod_kernel,
        grid=(b, n // tm),
        in_specs=[pl.BlockSpec((1, tm, d), lambda i, j: (i, j, 0)),
                  _gain_spec(gain_all, layer), _mod_spec(mod_all, layer, row_of)],
        out_specs=pl.BlockSpec((1, tm, d), lambda i, j: (i, j, 0)),
        out_shape=jax.ShapeDtypeStruct((b, n, d), BF16),
        compiler_params=_cp(("arbitrary", "arbitrary")),
        name="norm_mod",
    )(x, gain_all, mod_all)


def _proj_kernel(h_ref, w_ref, o_ref):
    o_ref[0] = _dot(h_ref[0], w_ref[0]).astype(o_ref.dtype)


def _proj(h, w_all, layer, col_off, n_cols, out_dtype):
    b, n, k = h.shape
    tm, tn = _tile(n, 1024), _tile(n_cols, 512)
    assert col_off % tn == 0
    c0 = col_off // tn
    return pl.pallas_call(
        _proj_kernel,
        grid=(b, n // tm, n_cols // tn),
        in_specs=[
            pl.BlockSpec((1, tm, k), lambda i, j, q: (i, j, 0)),
            pl.BlockSpec((1, k, tn), lambda i, j, q: (layer, 0, c0 + q)),
        ],
        out_specs=pl.BlockSpec((1, tm, tn), lambda i, j, q: (i, j, q)),
        out_shape=jax.ShapeDtypeStruct((b, n, n_cols), out_dtype),
        compiler_params=_cp(("arbitrary",) * 3),
        name="in_proj",
    )(h, w_all)


def _conv_proj_kernel(h_ref, wb_ref, wc_ref, wx_ref, ab_ref, u_ref):
    h = h_ref[0]
    ab_ref[0] = _dot(h, wb_ref[0])
    u_ref[0] = _dot(h, wc_ref[0]) * _dot(h, wx_ref[0])


def _conv_proj(h, w_all, layer, dc):
    b, n, k = h.shape
    tm, tn = _tile(n, 1024), _tile(dc, 256)
    nb = dc // tn
    out = jax.ShapeDtypeStruct((b, n, dc), F32)
    wblk = lambda off: pl.BlockSpec((1, k, tn), lambda i, j, q: (layer, 0, q + off))
    return pl.pallas_call(
        _conv_proj_kernel,
        grid=(b, n // tm, nb),
        in_specs=[pl.BlockSpec((1, tm, k), lambda i, j, q: (i, j, 0)), wblk(0), wblk(nb), wblk(2 * nb)],
        out_specs=[pl.BlockSpec((1, tm, tn), lambda i, j, q: (i, j, q))] * 2,
        out_shape=[out, out],
        compiler_params=_cp(("arbitrary",) * 3),
        name="conv_proj",
    )(h, w_all, w_all, w_all)


def _conv_kernel(ab_ref, u_ref, w_ref, o_ref):
    u = u_ref[0]
    n = u.shape[0]
    row = lax.broadcasted_iota(I32, u.shape, 0)
    prev = jnp.where(row == 0, 0.0, pltpu.roll(u, 1, 0))
    nxt = jnp.where(row == n - 1, 0.0, pltpu.roll(u, n - 1, 0))
    w = w_ref[0]
    o_ref[0] = (ab_ref[0] * (prev * w[0:1] + u * w[1:2] + nxt * w[2:3])).astype(BF16)


def _short_conv(ab, u, conv_w_all, layer):
    b, n, dc = u.shape
    tc = _tile(dc, LANE)
    blk = pl.BlockSpec((1, n, tc), lambda i, j: (i, 0, j))
    return pl.pallas_call(
        _conv_kernel,
        grid=(b, dc // tc),
        in_specs=[blk, blk, pl.BlockSpec((1, conv_w_all.shape[1], tc), lambda i, j: (layer, 0, j))],
        out_specs=blk,
        out_shape=jax.ShapeDtypeStruct((b, n, dc), BF16),
        compiler_params=_cp(("arbitrary", "arbitrary")),
        name="short_conv",
    )(ab, u, conv_w_all)


def _pool_kernel(u_ref, w_ref, s_ref, o_ref):
    grp = pl.program_id(1)
    u = u_ref[0]
    n, c = u.shape
    npad = n + 2 * POOL_PAD
    pad = jnp.zeros((POOL_PAD, c), F32)
    up = jnp.concatenate([pad, u, pad], axis=0)
    t = lax.broadcasted_iota(I32, (n, c), 0)

    def shifted(x, s):
        return pltpu.roll(x, s % npad, 0)

    for gi, win in enumerate(POOL_WINDOWS):
        @pl.when(grp == gi)
        def _(win=win):
            w = shifted(up, 1) + up
            span = 2
            while span < win:
                half = span // 2
                w = shifted(w, half) + shifted(w, -half)
                span *= 2
            ws = w[POOL_PAD:POOL_PAD + n]
            cnt = jnp.minimum(t + win // 2, n) - jnp.maximum(t - win // 2, 0)
            mixed = ws / cnt.astype(F32) - u
            y = _dot(mixed.astype(BF16), w_ref[0, 0]) * s_ref[0, 0]
            o_ref[0] = y.astype(BF16)


def _multiscale_pool(u, pool_w_all, pool_scale_all, layer):
    b, n, dp = u.shape
    g, pg = pool_w_all.shape[1], pool_w_all.shape[2]
    assert g == len(POOL_WINDOWS) and g * pg == dp
    return pl.pallas_call(
        _pool_kernel,
        grid=(b, g),
        in_specs=[
            pl.BlockSpec((1, n, pg), lambda i, j: (i, 0, j)),
            pl.BlockSpec((1, 1, pg, pg), lambda i, j: (layer, j, 0, 0)),
            pl.BlockSpec((1, 1, 1, pg), lambda i, j: (layer, j, 0, 0)),
        ],
        out_specs=pl.BlockSpec((1, n, pg), lambda i, j: (i, 0, j)),
        out_shape=jax.ShapeDtypeStruct((b, n, dp), BF16),
        compiler_params=_cp(("arbitrary", "arbitrary")),
        name="multiscale_pool",
    )(u, pool_w_all, pool_scale_all)


def _group_geometry(rows, win_rows):
    key_rows = Q_ROWS + win_rows
    n_groups = rows // Q_ROWS
    assert rows % Q_ROWS == 0 and n_groups >= 3 and rows >= key_rows
    return key_rows, n_groups, (0, 1, n_groups - 1)


def _bias_kernel(rpb_ref, o_ref, *, rows, win_rows, win_cols, n_heads):
    layer, head = pl.program_id(0), pl.program_id(1)
    key_rows, _, reps = _group_geometry(rows, win_rows)
    n_dr, n_dc = 2 * win_rows - 1, 2 * win_cols - 1
    base = (layer * n_heads + head) * n_dr * n_dc
    w = GRID_W
    qc = lax.broadcasted_iota(I32, (w, 2 * w), 0)
    lane = lax.broadcasted_iota(I32, (w, 2 * w), 1)
    kc = jnp.where(lane < w, lane, lane - w)
    first = lane < w
    cstart = jnp.clip(qc - win_cols // 2, 0, w - win_cols)
    in_cols = (kc >= cstart) & (kc < cstart + win_cols)
    dc = kc - qc + win_cols - 1

    def pair_block(dr_a, dr_b):
        acc = jnp.full((w, 2 * w), NEG, F32)
        if dr_a is None and dr_b is None:
            return acc
        for d in range(n_dc):
            va = rpb_ref[base + dr_a * n_dc + d] if dr_a is not None else NEG
            vb = rpb_ref[base + dr_b * n_dc + d] if dr_b is not None else NEG
            acc = jnp.where(in_cols & (dc == d), jnp.where(first, va, vb), acc)
        return acc

    for ti, g in enumerate(reps):
        kb = min(max(Q_ROWS * g - win_rows // 2, 0), rows - key_rows)
        for i in range(Q_ROWS):
            r = Q_ROWS * g + i
            rs = min(max(r - win_rows // 2, 0), rows - win_rows)
            drs = []
            for j in range(key_rows):
                kr = kb + j
                drs.append(kr - r + win_rows - 1 if rs <= kr < rs + win_rows else None)
            for jp in range(key_rows // 2):
                o_ref[0, 0, ti, i * w:(i + 1) * w, jp * 2 * w:(jp + 1) * 2 * w] = pair_block(
                    drs[2 * jp], drs[2 * jp + 1])


def _bias_tiles(rpb, rows):
    n_layers, n_heads, n_dr, n_dc = rpb.shape
    win_rows, win_cols = (n_dr + 1) // 2, (n_dc + 1) // 2
    key_rows, _, _ = _group_geometry(rows, win_rows)
    assert key_rows % 2 == 0
    tq, tk = Q_ROWS * GRID_W, key_rows * GRID_W
    return pl.pallas_call(
        functools.partial(_bias_kernel, rows=rows, win_rows=win_rows, win_cols=win_cols, n_heads=n_heads),
        grid=(n_layers, n_heads),
        in_specs=[pl.BlockSpec(memory_space=pltpu.SMEM)],
        out_specs=pl.BlockSpec((1, 1, 3, tq, tk), lambda l, h: (l, h, 0, 0, 0)),
        out_shape=jax.ShapeDtypeStruct((n_layers, n_heads, 3, tq, tk), F32),
        compiler_params=_cp(("arbitrary", "arbitrary")),
        name="bias_tiles",
    )(rpb.reshape(-1))


def _natten_kernel(q_ref, k_ref, v_ref, kc_ref, vc_ref, bias_ref, o_ref, *, rows, win_rows):
    key_rows, n_groups, _ = _group_geometry(rows, win_rows)
    tq, tk = Q_ROWS * GRID_W, key_rows * GRID_W
    scale = q_ref.shape[-1] ** -0.5
    kc = kc_ref[0]
    vc = vc_ref[0]

    def group(g, carry):
        kb = jnp.clip(Q_ROWS * g - win_rows // 2, 0, rows - key_rows)
        kind = jnp.where(g == 0, 0, jnp.where(g == n_groups - 1, 2, 1))
        q0 = pl.multiple_of(g * tq, tq)
        k0 = pl.multiple_of(kb * GRID_W, GRID_W)
        q = q_ref[0, pl.ds(q0, tq), :]
        kw = k_ref[0, pl.ds(k0, tk), :]
        vw = v_ref[0, pl.ds(k0, tk), :]
        s_win = _dot_nt(q, kw) * scale + bias_ref[0, 0, kind]
        s_ctx = _dot_nt(q, kc) * scale
        m = jnp.maximum(jnp.max(s_win, axis=-1, keepdims=True), jnp.max(s_ctx, axis=-1, keepdims=True))
        p_win = jnp.exp(s_win - m)
        p_ctx = jnp.exp(s_ctx - m)
        inv = 1.0 / (jnp.sum(p_win, axis=-1, keepdims=True) + jnp.sum(p_ctx, axis=-1, keepdims=True))
        o = (_dot(p_win.astype(BF16), vw) + _dot(p_ctx.astype(BF16), vc)) * inv
        o_ref[0, pl.ds(q0, tq), :] = o.astype(BF16)
        return carry

    lax.fori_loop(0, n_groups, group, 0, unroll=2)


def _natten(qkv, kv_ctx, k_off, v_off, bias_all, layer, n_heads):
    b, n, c3 = qkv.shape
    dh = c3 // (3 * n_heads)
    lc = kv_ctx.shape[1]
    rows = n // GRID_W
    win_rows = bias_all.shape[-1] // GRID_W - Q_ROWS
    seq = lambda off: pl.BlockSpec((1, n, dh), lambda i, h: (i, 0, off + h))
    ctx = lambda off: pl.BlockSpec((1, lc, dh), lambda i, h: (i, 0, off + h))
    return pl.pallas_call(
        functools.partial(_natten_kernel, rows=rows, win_rows=win_rows),
        grid=(b, n_heads),
        in_specs=[
            seq(0), seq(n_heads), seq(2 * n_heads), ctx(k_off), ctx(v_off),
            pl.BlockSpec((1, 1) + bias_all.shape[2:], lambda i, h: (layer, h, 0, 0, 0)),
        ],
        out_specs=pl.BlockSpec((1, n, dh), lambda i, h: (i, 0, h)),
        out_shape=jax.ShapeDtypeStruct((b, n, n_heads * dh), BF16),
        compiler_params=_cp(("arbitrary", "arbitrary")),
        name="natten",
    )(qkv, qkv, qkv, kv_ctx, kv_ctx, bias_all)


def _ctx_attn_kernel(q_ref, k_ref, v_ref, o_ref):
    q = q_ref[0]
    s = _dot_nt(q, k_ref[0]) * (q.shape[-1] ** -0.5)
    p = jnp.exp(s - jnp.max(s, axis=-1, keepdims=True))
    p = p * (1.0 / jnp.sum(p, axis=-1, keepdims=True))
    o_ref[0] = _dot(p.astype(BF16), v_ref[0]).astype(BF16)


def _ctx_attention(qkv, n_heads):
    b, lc, c3 = qkv.shape
    dh = c3 // (3 * n_heads)
    blk = lambda off: pl.BlockSpec((1, lc, dh), lambda i, h: (i, 0, off + h))
    return pl.pallas_call(
        _ctx_attn_kernel,
        grid=(b, n_heads),
        in_specs=[blk(0), blk(n_heads), blk(2 * n_heads)],
        out_specs=blk(0),
        out_shape=jax.ShapeDtypeStruct((b, lc, n_heads * dh), BF16),
        compiler_params=_cp(("arbitrary", "arbitrary")),
        name="ctx_attention",
    )(qkv, qkv, qkv)


def _merge_kernel(h_ref, zc_ref, zp_ref, za_ref, g0_ref, g1_ref, g2_ref, wc_ref, wp_ref, wa_ref, o_ref):
    h = h_ref[0]
    m = jax.nn.sigmoid(_dot(h, g0_ref[0])) * _dot(zc_ref[0], wc_ref[0])
    m = m + jax.nn.sigmoid(_dot(h, g1_ref[0])) * _dot(zp_ref[0], wp_ref[0])
    m = m + jax.nn.sigmoid(_dot(h, g2_ref[0])) * _dot(za_ref[0], wa_ref[0])
    o_ref[0] = m.astype(BF16)


def _merge(h, zc, zp, za, w_in_all, gate_off, wc_all, wp_all, wa_all, layer):
    b, n, d = h.shape
    tm, tn = _tile(n, 512), _tile(d, 512)
    nb = d // tn
    assert gate_off % tn == 0
    g0 = gate_off // tn
    act = lambda a: pl.BlockSpec((1, tm, a.shape[-1]), lambda i, j, q: (i, j, 0))
    wcol = lambda a, off: pl.BlockSpec((1, a.shape[1], tn), lambda i, j, q: (layer, 0, q + off))
    return pl.pallas_call(
        _merge_kernel,
        grid=(b, n // tm, nb),
        in_specs=[act(h), act(zc), act(zp), act(za),
                  wcol(w_in_all, g0), wcol(w_in_all, g0 + nb), wcol(w_in_all, g0 + 2 * nb),
                  wcol(wc_all, 0), wcol(wp_all, 0), wcol(wa_all, 0)],
        out_specs=pl.BlockSpec((1, tm, tn), lambda i, j, q: (i, j, q)),
        out_shape=jax.ShapeDtypeStruct((b, n, d), BF16),
        compiler_params=_cp(("arbitrary",) * 3, 56),
        name="merge",
    )(h, zc, zp, za, w_in_all, w_in_all, w_in_all, wc_all, wp_all, wa_all)


def _out_proj_kernel(m_ref, w_ref, x_ref, mod_ref, g_ref, x1_ref, h2_ref):
    mod = mod_ref[0, 0]
    x1 = x_ref[0] + mod[2:3] * _dot(m_ref[0], w_ref[0])
    x1_ref[0] = x1
    h2_ref[0] = _modulated_norm(x1, g_ref[0], mod, 3, 4).astype(BF16)


def _out_proj(merged, wo_all, x, mod_all, gain2_all, layer, row_of):
    b, n, d = x.shape
    tm = _tile(n, 256)
    row = pl.BlockSpec((1, tm, d), lambda i, j: (i, j, 0))
    return pl.pallas_call(
        _out_proj_kernel,
        grid=(b, n // tm),
        in_specs=[row, pl.BlockSpec((1, d, d), lambda i, j: (layer, 0, 0)), row,
                  _mod_spec(mod_all, layer, row_of), _gain_spec(gain2_all, layer)],
        out_specs=[row, row],
        out_shape=[jax.ShapeDtypeStruct((b, n, d), F32), jax.ShapeDtypeStruct((b, n, d), BF16)],
        compiler_params=_cp(("arbitrary", "arbitrary")),
        name="out_proj",
    )(merged, wo_all, x, mod_all, gain2_all)


def _router_kernel(h_ref, w_ref, o_ref):
    logits = _dot_nt(w_ref[0], h_ref[0])
    e = jnp.exp(logits - jnp.max(logits, axis=0, keepdims=True))
    o_ref[0] = e / jnp.sum(e, axis=0, keepdims=True)


def _router(h2, wr_t_all, layer):
    b, n, d = h2.shape
    e = wr_t_all.shape[1]
    tn = _tile(n, 1024)
    return pl.pallas_call(
        _router_kernel,
        grid=(b, n // tn),
        in_specs=[pl.BlockSpec((1, tn, d), lambda i, j: (i, j, 0)),
                  pl.BlockSpec((1, e, d), lambda i, j: (layer, 0, 0))],
        out_specs=pl.BlockSpec((1, e, tn), lambda i, j: (i, 0, j)),
        out_shape=jax.ShapeDtypeStruct((b, e, n), F32),
        compiler_params=_cp(("arbitrary", "arbitrary")),
        name="router",
    )(h2, wr_t_all)


def _lane_cumsum(x):
    e, n = x.shape
    nk = n // LANE
    i = lax.broadcasted_iota(I32, (LANE, LANE), 0)
    j = lax.broadcasted_iota(I32, (LANE, LANE), 1)
    upper = _onehot(i <= j)
    lane = lax.broadcasted_iota(I32, (e, LANE), 1)
    off = jnp.zeros((e, 1), F32)
    bounds = jnp.zeros((e, LANE), F32)
    parts = []
    for k in range(nk):
        bounds = jnp.where(lane == k, off, bounds)
        c = _dot(x[:, k * LANE:(k + 1) * LANE].astype(BF16), upper) + off
        parts.append(c)
        off = c[:, LANE - 1:LANE]
    bounds = jnp.where(lane == nk, off, bounds)
    return jnp.concatenate(parts, axis=1), bounds


def _route_kernel(aff_ref, pos_ref, cum_ref, kst_ref, *, cap, slot_blk, n_blk):
    aff = aff_ref[0]
    n_exp, n = aff.shape
    nk = n // LANE
    bits = pltpu.bitcast(aff, I32)

    def search(i, prefix):
        cand = prefix | jnp.left_shift(jnp.int32(1), 30 - i)
        cnt = jnp.sum(jnp.where(bits >= cand, 1.0, 0.0), axis=1, keepdims=True)
        return jnp.where(cnt >= cap, cand, prefix)

    thr = lax.fori_loop(0, 31, search, jnp.zeros((n_exp, 1), I32))
    gt = bits > thr
    eq = bits == thr
    need = cap - jnp.sum(jnp.where(gt, 1.0, 0.0), axis=1, keepdims=True)
    eq_f = jnp.where(eq, 1.0, 0.0)
    eq_rank, _ = _lane_cumsum(eq_f)
    sel = gt | (eq & (eq_rank - eq_f < need))
    sel_f = jnp.where(sel, 1.0, 0.0)
    rank, bounds = _lane_cumsum(sel_f)
    pos_ref[0] = jnp.where(sel, (rank - sel_f).astype(I32), -1)
    cum_ref[0] = bounds.astype(I32)
    lane = lax.broadcasted_iota(I32, (n_exp, LANE), 1)
    chunk_end = (lane >= 1) & (lane <= nk)
    kst = jnp.zeros((n_exp, LANE), F32)
    for blk in range(n_blk):
        below = jnp.where(chunk_end & (bounds <= blk * slot_blk), 1.0, 0.0)
        kst = jnp.where(lane == blk, jnp.sum(below, axis=1, keepdims=True), kst)
    kst_ref[0] = kst.astype(I32)


def _route(aff, cap, slot_blk, n_blk):
    b, e, n = aff.shape
    assert n % LANE == 0 and n // LANE < LANE and n_blk <= LANE
    tab = jax.ShapeDtypeStruct((b, e, LANE), I32)
    return pl.pallas_call(
        functools.partial(_route_kernel, cap=cap, slot_blk=slot_blk, n_blk=n_blk),
        grid=(b,),
        in_specs=[pl.BlockSpec((1, e, n), lambda i: (i, 0, 0))],
        out_specs=[pl.BlockSpec((1, e, n), lambda i: (i, 0, 0)),
                   pl.BlockSpec((1, e, LANE), lambda i: (i, 0, 0)),
                   pl.BlockSpec((1, e, LANE), lambda i: (i, 0, 0))],
        out_shape=[jax.ShapeDtypeStruct((b, e, n), I32), tab, tab],
        compiler_params=_cp(("arbitrary",)),
        name="route",
    )(aff)


def _gather_kernel(cum_ref, kst_ref, h_ref, pos_ref, aff_ref, xs_ref, gs_ref, acc_ref, gacc_ref,
                   *, n_exp, slot_blk, win_chunks):
    b, e = pl.program_id(0), pl.program_id(1)
    nk = pos_ref.shape[2]
    capp = acc_ref.shape[0]
    base = (b * n_exp + e) * LANE
    for blk in range(capp // slot_blk):
        s0 = blk * slot_blk
        rows = slice(s0, s0 + slot_blk)
        slot = lax.broadcasted_iota(I32, (slot_blk, LANE), 0) + s0
        k0 = jnp.minimum(kst_ref[base + blk], nk - win_chunks)
        hits = [pos_ref[0, e, pl.ds(k0 + j, 1), :] == slot for j in range(win_chunks)]
        t0 = pl.multiple_of(k0 * LANE, LANE)
        acc_ref[rows, :] = _dot(jnp.concatenate([_onehot(h) for h in hits], axis=1),
                                h_ref[0, pl.ds(t0, win_chunks * LANE), :])
        g = jnp.zeros((slot_blk, 1), F32)
        for j in range(win_chunks):
            g = g + jnp.sum(jnp.where(hits[j], aff_ref[0, e, pl.ds(k0 + j, 1), :], 0.0), axis=1, keepdims=True)
        gacc_ref[rows, :] = g

        def tail(k, carry):
            @pl.when(cum_ref[base + k] < s0 + slot_blk)
            def _():
                hit = pos_ref[0, e, pl.ds(k, 1), :] == slot
                tk = pl.multiple_of(k * LANE, LANE)
                acc_ref[rows, :] += _dot(_onehot(hit), h_ref[0, pl.ds(tk, LANE), :])
                gacc_ref[rows, :] += jnp.sum(
                    jnp.where(hit, aff_ref[0, e, pl.ds(k, 1), :], 0.0), axis=1, keepdims=True)

            return carry

        lax.fori_loop(k0 + win_chunks, nk, tail, 0)
    xs_ref[0] = acc_ref[...].astype(BF16)
    gs_ref[0] = gacc_ref[...]


def _gather(h2, pos, aff, cum, kst, capp, slot_blk):
    b, n, d = h2.shape
    e = pos.shape[1]
    nk = n // LANE
    win_chunks = min(GATHER_CHUNKS, nk)
    chunked = pl.BlockSpec((1, e, nk, LANE), lambda i, j, *_: (i, 0, 0, 0))
    grid_spec = pltpu.PrefetchScalarGridSpec(
        num_scalar_prefetch=2,
        grid=(b, e),
        in_specs=[pl.BlockSpec((1, n, d), lambda i, j, *_: (i, 0, 0)), chunked, chunked],
        out_specs=[pl.BlockSpec((1, capp, d), lambda i, j, *_: (j, i, 0)),
                   pl.BlockSpec((1, capp, 1), lambda i, j, *_: (j, i, 0))],
        scratch_shapes=[pltpu.VMEM((capp, d), F32), pltpu.VMEM((capp, 1), F32)],
    )
    return pl.pallas_call(
        functools.partial(_gather_kernel, n_exp=e, slot_blk=slot_blk, win_chunks=win_chunks),
        grid_spec=grid_spec,
        out_shape=[jax.ShapeDtypeStruct((e, b * capp, d), BF16),
                   jax.ShapeDtypeStruct((e, b * capp, 1), F32)],
        compiler_params=_cp(("arbitrary", "arbitrary"), 56),
        name="moe_gather",
    )(cum.reshape(-1), kst.reshape(-1), h2, pos.reshape(b, e, nk, LANE), aff.reshape(b, e, nk, LANE))


def _expert_up_kernel(*refs, n_in):
    x_refs, (wg_ref, wu_ref), o_refs = refs[:n_in], refs[n_in:n_in + 2], refs[n_in + 2:]
    wg = wg_ref[0, 0].astype(BF16)
    wu = wu_ref[0, 0].astype(BF16)
    for x_ref, o_ref in zip(x_refs, o_refs):
        x = x_ref[0]
        g = _dot(x, wg)
        o_ref[0] = (g * jax.nn.sigmoid(g) * _dot(x, wu)).astype(BF16)


def _expert_up(xs_list, w_gate_all, w_up_all, layer):
    e, _, d = xs_list[0].shape
    f = w_gate_all.shape[-1]
    tf = _tile(f, 256)
    wblk = pl.BlockSpec((1, 1, d, tf), lambda i, j: (layer, i, 0, j))
    return pl.pallas_call(
        functools.partial(_expert_up_kernel, n_in=len(xs_list)),
        grid=(e, f // tf),
        in_specs=[pl.BlockSpec((1, x.shape[1], d), lambda i, j: (i, 0, 0)) for x in xs_list] + [wblk, wblk],
        out_specs=[pl.BlockSpec((1, x.shape[1], tf), lambda i, j: (i, 0, j)) for x in xs_list],
        out_shape=[jax.ShapeDtypeStruct((e, x.shape[1], f), BF16) for x in xs_list],
        compiler_params=_cp(("arbitrary", "arbitrary"), 56),
        name="expert_up",
    )(*xs_list, w_gate_all, w_up_all)


def _expert_down_kernel(*refs, n_in):
    h_refs, g_refs, w_ref, o_refs = refs[:n_in], refs[n_in:2 * n_in], refs[2 * n_in], refs[2 * n_in + 1:]
    w = w_ref[0, 0].astype(BF16)
    for h_ref, g_ref, o_ref in zip(h_refs, g_refs, o_refs):
        o_ref[0] = (_dot(h_ref[0], w) * g_ref[0]).astype(BF16)


def _expert_down(hid_list, gs_list, w_down_all, layer):
    e, _, f = hid_list[0].shape
    d = w_down_all.shape[-1]
    tn = _tile(d, 256)
    return pl.pallas_call(
        functools.partial(_expert_down_kernel, n_in=len(hid_list)),
        grid=(e, d // tn),
        in_specs=[pl.BlockSpec((1, h.shape[1], f), lambda i, j: (i, 0, 0)) for h in hid_list]
        + [pl.BlockSpec((1, g.shape[1], 1), lambda i, j: (i, 0, 0)) for g in gs_list]
        + [pl.BlockSpec((1, 1, f, tn), lambda i, j: (layer, i, 0, j))],
        out_specs=[pl.BlockSpec((1, h.shape[1], tn), lambda i, j: (i, 0, j)) for h in hid_list],
        out_shape=[jax.ShapeDtypeStruct((e, h.shape[1], d), BF16) for h in hid_list],
        compiler_params=_cp(("arbitrary", "arbitrary"), 56),
        name="expert_down",
    )(*hid_list, *gs_list, w_down_all)


def _combine_kernel(cum_ref, pos_ref, x_ref, mod_ref, gain_ref, modn_ref, ys_hbm, *rest,
                    n_exp, capp, win, chunks_per_step, final):
    n_out = 1 if final else 2
    outs, (ybuf, xbuf, acc_ref, sem, xsem) = rest[:n_out], rest[n_out:]
    b, t = pl.program_id(0), pl.program_id(1)
    n_t = pl.num_programs(1)
    step = b * n_t + t
    n_steps = pl.num_programs(0) * n_t
    slot = lax.rem(step, 2)
    tokens = pos_ref.shape[2]
    slot_iota = lax.broadcasted_iota(I32, (win, tokens), 0)

    def win_start(bb, tt, e):
        lo = cum_ref[(bb * n_exp + e) * LANE + tt * chunks_per_step]
        return jnp.minimum((lo // BF16_ROWS) * BF16_ROWS, capp - win)

    def window_copy(bb, tt, e, s):
        src0 = pl.multiple_of(bb * capp + win_start(bb, tt, e), BF16_ROWS)
        return pltpu.make_async_copy(ys_hbm.at[e, pl.ds(src0, win), :],
                                     ybuf.at[s, pl.ds(e * win, win), :], sem.at[s])

    def start_windows(bb, tt, s):
        for e in range(n_exp):
            window_copy(bb, tt, e, s).start()

    @pl.when(step == 0)
    def _():
        start_windows(b, t, slot)

    @pl.when(step + 1 < n_steps)
    def _():
        nxt = step + 1
        start_windows(nxt // n_t, lax.rem(nxt, n_t), 1 - slot)

    onehot = jnp.concatenate(
        [_onehot(pos_ref[0, e:e + 1, :] == slot_iota + win_start(b, t, e)) for e in range(n_exp)], axis=0)
    for e in range(n_exp):
        window_copy(b, t, e, slot).wait()
    acc_ref[...] = _dot_tn(onehot, ybuf[slot])

    def long_run(e, carry):
        ws = win_start(b, t, e)
        hi = cum_ref[(b * n_exp + e) * LANE + (t + 1) * chunks_per_step]

        def extra(j, c):
            first = ws + j * win
            wj = jnp.minimum(first, capp - win)
            src0 = pl.multiple_of(b * capp + wj, BF16_ROWS)
            cp = pltpu.make_async_copy(ys_hbm.at[e, pl.ds(src0, win), :], xbuf, xsem)
            cp.start()
            cp.wait()
            sl = slot_iota + wj
            hit = (pos_ref[0, pl.ds(e, 1), :] == sl) & (sl >= first)
            acc_ref[...] += _dot_tn(_onehot(hit), xbuf[...])
            return c

        lax.fori_loop(1, (hi - ws + win - 1) // win, extra, 0)
        return carry

    lax.fori_loop(0, n_exp, long_run, 0)

    y = x_ref[0] + mod_ref[0, 0][5:6] * acc_ref[...]
    if final:
        outs[0][0] = _rms(y, gain_ref[0])
    else:
        outs[0][0] = y
        outs[1][0] = _modulated_norm(y, gain_ref[0], modn_ref[0, 0], 0, 1).astype(BF16)


def _combine(ys, pos, cum, x1, mod_all, layer, row_of, capp, gain_all, gain_layer, final):
    b, n, d = x1.shape
    e = pos.shape[1]
    tokens = min(COMBINE_TOKENS, n)
    win = min(COMBINE_WIN, capp)
    assert n % tokens == 0 and tokens % LANE == 0 and capp % BF16_ROWS == 0 and win % BF16_ROWS == 0
    mod_next_layer = layer if final else layer + 1
    row = pl.BlockSpec((1, tokens, d), lambda i, t, *_: (i, t, 0))
    grid_spec = pltpu.PrefetchScalarGridSpec(
        num_scalar_prefetch=1,
        grid=(b, n // tokens),
        in_specs=[
            pl.BlockSpec((1, e, tokens), lambda i, t, *_: (i, 0, t)),
            row,
            _mod_spec(mod_all, layer, row_of),
            _gain_spec(gain_all, gain_layer),
            _mod_spec(mod_all, mod_next_layer, row_of),
            pl.BlockSpec(memory_space=pl.ANY),
        ],
        out_specs=[row] if final else [row, row],
        scratch_shapes=[pltpu.VMEM((2, e * win, d), BF16), pltpu.VMEM((win, d), BF16),
                        pltpu.VMEM((tokens, d), F32),
                        pltpu.SemaphoreType.DMA((2,)), pltpu.SemaphoreType.DMA],
    )
    out_f32 = jax.ShapeDtypeStruct((b, n, d), F32)
    return pl.pallas_call(
        functools.partial(_combine_kernel, n_exp=e, capp=capp, win=win,
                          chunks_per_step=tokens // LANE, final=final),
        grid_spec=grid_spec,
        out_shape=[out_f32] if final else [out_f32, jax.ShapeDtypeStruct((b, n, d), BF16)],
        compiler_params=_cp(("arbitrary", "arbitrary")),
        name="moe_combine",
    )(cum.reshape(-1), pos, x1, mod_all, gain_all, mod_all, ys)


def _route_and_gather(h2, wr_t_all, layer):
    b, n, _ = h2.shape
    e = wr_t_all.shape[1]
    cap = max(1, CAPACITY_FACTOR * n // e)
    unit = SLOT_BLK if cap > SLOT_BLK else BF16_ROWS
    capp = -(-cap // unit) * unit
    slot_blk = min(SLOT_BLK, capp)
    aff = _router(h2, wr_t_all, layer)
    pos, cum, kst = _route(aff, cap, slot_blk, capp // slot_blk)
    xs, gs = _gather(h2, pos, aff, cum, kst, capp, slot_blk)
    return dict(xs=xs, gs=gs, pos=pos, cum=cum, capp=capp)


def kernel(x, c, ctx, c_ctx, w_mod, b_mod, norm1, norm2, w_in, conv_w, pool_w, pool_scale, rpb,
           w_conv_out, w_pool_out, w_attn_out, w_o, w_router, w_e_gate, w_e_up, w_e_down, final_norm):
    depth = w_mod.shape[0]
    b, n, d = x.shape
    dc, dp = conv_w.shape[-1], pool_scale.shape[-1]
    n_heads, da = rpb.shape[1], w_attn_out.shape[1]
    off_pool, off_q = 3 * dc, 3 * dc + dp
    off_k, off_gate = off_q + da, off_q + 3 * da
    assert n % GRID_W == 0 and da % n_heads == 0

    n_rows = -(-(b + 1) // SUBLANE) * SUBLANE
    cond = jnp.zeros((n_rows, d), F32).at[:b].set(c).at[b].set(c_ctx)
    mod_all = _adaln_all(cond, w_mod, b_mod).reshape(depth, n_rows, N_MOD, d)
    lat_row, ctx_row = (lambda i: i), (lambda i: b)
    bias_all = _bias_tiles(rpb, n // GRID_W)

    w_in_b = w_in.astype(BF16)
    wc_b, wp_b, wa_b, wo_b = (w.astype(BF16) for w in (w_conv_out, w_pool_out, w_attn_out, w_o))
    wr_t = jnp.swapaxes(w_router, 1, 2).astype(BF16)
    pool_w_b = pool_w.astype(BF16)
    pool_scale_r = pool_scale.reshape(depth, pool_w.shape[1], 1, -1)
    gain1, gain2 = norm1.reshape(depth, 1, d), norm2.reshape(depth, 1, d)
    gain_f = final_norm.reshape(1, 1, d)

    def mixer(h, attn, l):
        ab, u = _conv_proj(h, w_in_b, l, dc)
        zc = _short_conv(ab, u, conv_w, l)
        zp = _multiscale_pool(_proj(h, w_in_b, l, off_pool, dp, F32), pool_w_b, pool_scale_r, l)
        return _merge(h, zc, zp, attn, w_in_b, off_gate, wc_b, wp_b, wa_b, l)

    hx = _norm_mod(x, gain1, mod_all, 0, lat_row)
    hc = _norm_mod(ctx, gain1, mod_all, 0, ctx_row)
    for l in range(depth):
        last = l == depth - 1
        groups = []
        if last:
            kv_c = _proj(hc, w_in_b, l, off_k, 2 * da, BF16)
            k_off, v_off = 0, n_heads
        else:
            kv_c = _proj(hc, w_in_b, l, off_q, 3 * da, BF16)
            k_off, v_off = n_heads, 2 * n_heads
            merged_c = mixer(hc, _ctx_attention(kv_c, n_heads), l)
            ctx1, hc2 = _out_proj(merged_c, wo_b, ctx, mod_all, gain2, l, ctx_row)
            groups.append(dict(_route_and_gather(hc2, wr_t, l), x1=ctx1, row_of=ctx_row))

        qkv = _proj(hx, w_in_b, l, off_q, 3 * da, BF16)
        attn = _natten(qkv, kv_c, k_off, v_off, bias_all, l, n_heads)
        x1, hx2 = _out_proj(mixer(hx, attn, l), wo_b, x, mod_all, gain2, l, lat_row)
        groups.append(dict(_route_and_gather(hx2, wr_t, l), x1=x1, row_of=lat_row))

        hid = _expert_up([g["xs"] for g in groups], w_e_gate, w_e_up, l)
        ys = _expert_down(hid, [g["gs"] for g in groups], w_e_down, l)
        res = [_combine(y, g["pos"], g["cum"], g["x1"], mod_all, l, g["row_of"], g["capp"],
                        gain_f if last else gain1, 0 if last else l + 1, last)
               for y, g in zip(ys, groups)]
        if last:
            return res[-1][0]
        (ctx, hc), (x, hx) = res
```

```python
import functools

import jax
import jax.numpy as jnp
from jax import lax
from jax.experimental import pallas as pl
from jax.experimental.pallas import tpu as pltpu

F32 = jnp.float32
BF16 = jnp.bfloat16
I32 = jnp.int32

GRID_W = 64
POOL_WINDOWS = (2, 4, 8, 16)
CAPACITY_FACTOR = 2
N_MOD = 6
EPS = 1e-6
NEG = -1e30

LANE = 128
SUBLANE = 8
BF16_ROWS = 16
MIB = 1024 * 1024
Q_ROWS = 4
HALO = 16
GATHER_TOKENS = 256
GATHER_WIN = 80
EXPERT_GROUP = 8
LOG2E = 1.4426950408889634
COMBINE_TOKENS = 256
COMBINE_WIN = 64


def _cp(sem, vmem_mib=48):
    return pltpu.CompilerParams(dimension_semantics=sem, vmem_limit_bytes=vmem_mib * MIB)


def _dot(a, b):
    return jnp.dot(a, b, preferred_element_type=F32)


def _dot_nt(a, b):
    return lax.dot_general(a, b, (((1,), (1,)), ((), ())), preferred_element_type=F32)


def _dot_tn(a, b):
    return lax.dot_general(a, b, (((0,), (0,)), ((), ())), preferred_element_type=F32)


def _tile(n, pref):
    return pref if n % pref == 0 else n


def _onehot(hit):
    return jnp.where(hit, 1.0, 0.0).astype(BF16)


def _mod_kernel(c_ref, w_ref, b_ref, o_ref):
    c = c_ref[...]
    a = (c * jax.nn.sigmoid(c)).astype(BF16)
    o_ref[0] = _dot(a, w_ref[0].astype(BF16)) + b_ref[0]


def _adaln_all(cond, w_mod, b_mod):
    n_layers, d, cols = w_mod.shape
    r = cond.shape[0]
    tn = _tile(cols, 1024)
    return pl.pallas_call(
        _mod_kernel,
        grid=(n_layers, cols // tn),
        in_specs=[
            pl.BlockSpec((r, d), lambda l, j: (0, 0)),
            pl.BlockSpec((1, d, tn), lambda l, j: (l, 0, j)),
            pl.BlockSpec((1, 1, tn), lambda l, j: (l, 0, j)),
        ],
        out_specs=pl.BlockSpec((1, r, tn), lambda l, j: (l, 0, j)),
        out_shape=jax.ShapeDtypeStruct((n_layers, r, cols), F32),
        compiler_params=_cp(("arbitrary", "arbitrary")),
        name="adaln_mod",
    )(cond, w_mod, b_mod.reshape(n_layers, 1, cols))


def _mod_spec(mod_all, layer, row_of):
    return pl.BlockSpec((1, 1) + mod_all.shape[2:], lambda i, *_: (layer, row_of(i), 0, 0))


def _gain_spec(gain_all, layer):
    return pl.BlockSpec((1, 1, gain_all.shape[-1]), lambda *_: (layer, 0, 0))


def _rms(x, g):
    return x * lax.rsqrt(jnp.mean(x * x, axis=-1, keepdims=True) + EPS) * g


def _modulated_norm(x, gain, mod, shift_i, scale_i):
    return _rms(x, gain) * (1.0 + mod[scale_i:scale_i + 1]) + mod[shift_i:shift_i + 1]


def _norm_mod_kernel(x_ref, g_ref, mod_ref, o_ref):
    o_ref[0] = _modulated_norm(x_ref[0], g_ref[0], mod_ref[0, 0], 0, 1).astype(BF16)


def _norm_mod(x, gain_all, mod_all, layer, row_of):
    b, n, d = x.shape
    tm = _tile(n, 512)
    return pl.pallas_call(
        _norm_mod_kernel,
        grid=(b, n // tm),
        in_specs=[pl.BlockSpec((1, tm, d), lambda i, j: (i, j, 0)),
                  _gain_spec(gain_all, layer), _mod_spec(mod_all, layer, row_of)],
        out_specs=pl.BlockSpec((1, tm, d), lambda i, j: (i, j, 0)),
        out_shape=jax.ShapeDtypeStruct((b, n, d), BF16),
        compiler_params=_cp(("arbitrary", "arbitrary")),
        name="norm_mod",
    )(x, gain_all, mod_all)


def _proj_kernel(h_ref, w_ref, o_ref):
    o_ref[0] = _dot(h_ref[0], w_ref[0]).astype(o_ref.dtype)


def _proj(h, w_all, layer, col_off, n_cols, out_dtype):
    b, n, k = h.shape
    tm = _tile(n, 1024)
    tn = next(t for t in (1024, 512, 256, LANE) if n_cols % t == 0 and col_off % t == 0)
    c0 = col_off // tn
    return pl.pallas_call(
        _proj_kernel,
        grid=(b, n // tm, n_cols // tn),
        in_specs=[
            pl.BlockSpec((1, tm, k), lambda i, j, q: (i, j, 0)),
            pl.BlockSpec((1, k, tn), lambda i, j, q: (layer, 0, c0 + q)),
        ],
        out_specs=pl.BlockSpec((1, tm, tn), lambda i, j, q: (i, j, q)),
        out_shape=jax.ShapeDtypeStruct((b, n, n_cols), out_dtype),
        compiler_params=_cp(("arbitrary",) * 3),
        name="in_proj",
    )(h, w_all)


def _halo_specs(n, tm, k):
    assert tm % HALO == 0 and max(POOL_WINDOWS) // 2 <= HALO // 2
    per, last = tm // HALO, n // HALO - 1
    return [pl.BlockSpec((1, HALO, k), lambda i, j, q: (i, jnp.maximum(j * per - 1, 0), 0)),
            pl.BlockSpec((1, tm, k), lambda i, j, q: (i, j, 0)),
            pl.BlockSpec((1, HALO, k), lambda i, j, q: (i, jnp.minimum((j + 1) * per, last), 0))]


def _project_with_halo(hp_ref, h_ref, hn_ref, project):
    tm = h_ref.shape[1]
    u = project(jnp.concatenate([hp_ref[0], h_ref[0], hn_ref[0]], axis=0))
    j, nj = pl.program_id(1), pl.num_programs(1)
    row = lax.broadcasted_iota(I32, u.shape, 0)
    outside = ((j == 0) & (row < HALO)) | ((j == nj - 1) & (row >= tm + HALO))
    return jnp.where(outside, 0.0, u)


def _conv_branch_kernel(hp_ref, h_ref, hn_ref, wb_ref, wc_ref, wx_ref, cw_ref, o_ref):
    tm = h_ref.shape[1]
    u = _project_with_halo(hp_ref, h_ref, hn_ref, lambda a: _dot(a, wc_ref[0]) * _dot(a, wx_ref[0]))
    rows = u.shape[0]
    centre = slice(HALO, HALO + tm)
    w = cw_ref[0]
    conv = (pltpu.roll(u, 1, 0)[centre] * w[0:1] + u[centre] * w[1:2]
            + pltpu.roll(u, rows - 1, 0)[centre] * w[2:3])
    o_ref[0] = (_dot(h_ref[0], wb_ref[0]) * conv).astype(BF16)


def _conv_branch(h, w_all, conv_w_all, layer, dc):
    b, n, k = h.shape
    tm, tn = _tile(n, 1024), _tile(dc, 512)
    nb = dc // tn
    wblk = lambda off: pl.BlockSpec((1, k, tn), lambda i, j, q: (layer, 0, q + off))
    return pl.pallas_call(
        _conv_branch_kernel,
        grid=(b, n // tm, nb),
        in_specs=_halo_specs(n, tm, k) + [wblk(0), wblk(nb), wblk(2 * nb),
                                          pl.BlockSpec((1, conv_w_all.shape[1], tn), lambda i, j, q: (layer, 0, q))],
        out_specs=pl.BlockSpec((1, tm, tn), lambda i, j, q: (i, j, q)),
        out_shape=jax.ShapeDtypeStruct((b, n, dc), BF16),
        compiler_params=_cp(("arbitrary",) * 3, 56),
        name="conv_branch",
    )(h, h, h, w_all, w_all, w_all, conv_w_all)


def _pool_branch_kernel(hp_ref, h_ref, hn_ref, wi_ref, pw_ref, s_ref, o_ref, *, n):
    grp = pl.program_id(2)
    tm = h_ref.shape[1]
    u = _project_with_halo(hp_ref, h_ref, hn_ref, lambda a: _dot(a, wi_ref[0]))
    rows = u.shape[0]
    centre = slice(HALO, HALO + tm)
    t = lax.broadcasted_iota(I32, (tm, u.shape[1]), 0) + pl.program_id(1) * tm

    def shifted(x, s):
        return pltpu.roll(x, s % rows, 0)

    for gi, win in enumerate(POOL_WINDOWS):
        @pl.when(grp == gi)
        def _(win=win):
            w = shifted(u, 1) + u
            span = 2
            while span < win:
                half = span // 2
                w = shifted(w, half) + shifted(w, -half)
                span *= 2
            cnt = jnp.minimum(t + win // 2, n) - jnp.maximum(t - win // 2, 0)
            mixed = w[centre] / cnt.astype(F32) - u[centre]
            y = _dot(mixed.astype(BF16), pw_ref[0, 0]) * s_ref[0, 0]
            o_ref[0] = y.astype(BF16)


def _pool_branch(h, w_all, col_off, pool_w_all, pool_scale_all, layer):
    b, n, k = h.shape
    g, pg = pool_w_all.shape[1], pool_w_all.shape[2]
    assert g == len(POOL_WINDOWS) and col_off % pg == 0
    c0 = col_off // pg
    tm = _tile(n, 1024)
    return pl.pallas_call(
        functools.partial(_pool_branch_kernel, n=n),
        grid=(b, n // tm, g),
        in_specs=_halo_specs(n, tm, k) + [
            pl.BlockSpec((1, k, pg), lambda i, j, q: (layer, 0, c0 + q)),
            pl.BlockSpec((1, 1, pg, pg), lambda i, j, q: (layer, q, 0, 0)),
            pl.BlockSpec((1, 1, 1, pg), lambda i, j, q: (layer, q, 0, 0)),
        ],
        out_specs=pl.BlockSpec((1, tm, pg), lambda i, j, q: (i, j, q)),
        out_shape=jax.ShapeDtypeStruct((b, n, g * pg), BF16),
        compiler_params=_cp(("arbitrary",) * 3),
        name="pool_branch",
    )(h, h, h, w_all, pool_w_all, pool_scale_all)


def _group_geometry(rows, win_rows):
    key_rows = Q_ROWS + win_rows
    n_groups = rows // Q_ROWS
    assert rows % Q_ROWS == 0 and n_groups >= 3 and rows >= key_rows
    return key_rows, n_groups, (0, 1, n_groups - 1)


def _bias_kernel(rpb_ref, o_ref, *, rows, win_rows, win_cols, n_heads):
    layer, head = pl.program_id(0), pl.program_id(1)
    key_rows, _, reps = _group_geometry(rows, win_rows)
    n_dr, n_dc = 2 * win_rows - 1, 2 * win_cols - 1
    base = (layer * n_heads + head) * n_dr * n_dc
    w = GRID_W
    qc = lax.broadcasted_iota(I32, (w, 2 * w), 0)
    lane = lax.broadcasted_iota(I32, (w, 2 * w), 1)
    kc = jnp.where(lane < w, lane, lane - w)
    first = lane < w
    cstart = jnp.clip(qc - win_cols // 2, 0, w - win_cols)
    in_cols = (kc >= cstart) & (kc < cstart + win_cols)
    dc = kc - qc + win_cols - 1

    def pair_block(dr_a, dr_b):
        acc = jnp.full((w, 2 * w), NEG, F32)
        if dr_a is None and dr_b is None:
            return acc
        for d in range(n_dc):
            va = rpb_ref[base + dr_a * n_dc + d] if dr_a is not None else NEG
            vb = rpb_ref[base + dr_b * n_dc + d] if dr_b is not None else NEG
            acc = jnp.where(in_cols & (dc == d), jnp.where(first, va, vb), acc)
        return acc * LOG2E

    for ti, g in enumerate(reps):
        kb = min(max(Q_ROWS * g - win_rows // 2, 0), rows - key_rows)
        for i in range(Q_ROWS):
            r = Q_ROWS * g + i
            rs = min(max(r - win_rows // 2, 0), rows - win_rows)
            drs = []
            for j in range(key_rows):
                kr = kb + j
                drs.append(kr - r + win_rows - 1 if rs <= kr < rs + win_rows else None)
            for jp in range(key_rows // 2):
                o_ref[0, 0, ti, i * w:(i + 1) * w, jp * 2 * w:(jp + 1) * 2 * w] = pair_block(
                    drs[2 * jp], drs[2 * jp + 1])


def _bias_tiles(rpb, rows):
    n_layers, n_heads, n_dr, n_dc = rpb.shape
    win_rows, win_cols = (n_dr + 1) // 2, (n_dc + 1) // 2
    key_rows, _, _ = _group_geometry(rows, win_rows)
    assert key_rows % 2 == 0
    tq, tk = Q_ROWS * GRID_W, key_rows * GRID_W
    return pl.pallas_call(
        functools.partial(_bias_kernel, rows=rows, win_rows=win_rows, win_cols=win_cols, n_heads=n_heads),
        grid=(n_layers, n_heads),
        in_specs=[pl.BlockSpec(memory_space=pltpu.SMEM)],
        out_specs=pl.BlockSpec((1, 1, 3, tq, tk), lambda l, h: (l, h, 0, 0, 0)),
        out_shape=jax.ShapeDtypeStruct((n_layers, n_heads, 3, tq, tk), F32),
        compiler_params=_cp(("arbitrary", "arbitrary")),
        name="bias_tiles",
    )(rpb.reshape(-1))


def _natten_kernel(q_ref, k_ref, v_ref, kc_ref, vc_ref, bias_ref, o_ref, *, rows, win_rows):
    key_rows, n_groups, _ = _group_geometry(rows, win_rows)
    tq, tk = Q_ROWS * GRID_W, key_rows * GRID_W
    scale = q_ref.shape[-1] ** -0.5 * LOG2E
    kc = kc_ref[0]
    vc = vc_ref[0]

    def group(g, carry):
        kb = jnp.clip(Q_ROWS * g - win_rows // 2, 0, rows - key_rows)
        kind = jnp.where(g == 0, 0, jnp.where(g == n_groups - 1, 2, 1))
        q0 = pl.multiple_of(g * tq, tq)
        k0 = pl.multiple_of(kb * GRID_W, GRID_W)
        q = q_ref[0, pl.ds(q0, tq), :]
        kw = k_ref[0, pl.ds(k0, tk), :]
        vw = v_ref[0, pl.ds(k0, tk), :]
        s_win = _dot_nt(q, kw) * scale + bias_ref[0, 0, kind]
        s_ctx = _dot_nt(q, kc) * scale
        m = jnp.maximum(jnp.max(s_win, axis=-1, keepdims=True), jnp.max(s_ctx, axis=-1, keepdims=True))
        p_win = jnp.exp2(s_win - m)
        p_ctx = jnp.exp2(s_ctx - m)
        inv = 1.0 / (jnp.sum(p_win, axis=-1, keepdims=True) + jnp.sum(p_ctx, axis=-1, keepdims=True))
        o = (_dot(p_win.astype(BF16), vw) + _dot(p_ctx.astype(BF16), vc)) * inv
        o_ref[0, pl.ds(q0, tq), :] = o.astype(BF16)
        return carry

    lax.fori_loop(0, n_groups, group, 0, unroll=2)


def _natten(qkv, kv_ctx, k_off, v_off, bias_all, layer, n_heads):
    b, n, c3 = qkv.shape
    dh = c3 // (3 * n_heads)
    lc = kv_ctx.shape[1]
    rows = n // GRID_W
    win_rows = bias_all.shape[-1] // GRID_W - Q_ROWS
    seq = lambda off: pl.BlockSpec((1, n, dh), lambda i, h: (i, 0, off + h))
    ctx = lambda off: pl.BlockSpec((1, lc, dh), lambda i, h: (i, 0, off + h))
    return pl.pallas_call(
        functools.partial(_natten_kernel, rows=rows, win_rows=win_rows),
        grid=(b, n_heads),
        in_specs=[
            seq(0), seq(n_heads), seq(2 * n_heads), ctx(k_off), ctx(v_off),
            pl.BlockSpec((1, 1) + bias_all.shape[2:], lambda i, h: (layer, h, 0, 0, 0)),
        ],
        out_specs=pl.BlockSpec((1, n, dh), lambda i, h: (i, 0, h)),
        out_shape=jax.ShapeDtypeStruct((b, n, n_heads * dh), BF16),
        compiler_params=_cp(("arbitrary", "arbitrary")),
        name="natten",
    )(qkv, qkv, qkv, kv_ctx, kv_ctx, bias_all)


def _ctx_attn_kernel(q_ref, k_ref, v_ref, o_ref):
    q = q_ref[0]
    s = _dot_nt(q, k_ref[0]) * (q.shape[-1] ** -0.5)
    p = jnp.exp(s - jnp.max(s, axis=-1, keepdims=True))
    p = p * (1.0 / jnp.sum(p, axis=-1, keepdims=True))
    o_ref[0] = _dot(p.astype(BF16), v_ref[0]).astype(BF16)


def _ctx_attention(qkv, n_heads):
    b, lc, c3 = qkv.shape
    dh = c3 // (3 * n_heads)
    blk = lambda off: pl.BlockSpec((1, lc, dh), lambda i, h: (i, 0, off + h))
    return pl.pallas_call(
        _ctx_attn_kernel,
        grid=(b, n_heads),
        in_specs=[blk(0), blk(n_heads), blk(2 * n_heads)],
        out_specs=blk(0),
        out_shape=jax.ShapeDtypeStruct((b, lc, n_heads * dh), BF16),
        compiler_params=_cp(("arbitrary", "arbitrary")),
        name="ctx_attention",
    )(qkv, qkv, qkv)


def _merge_kernel(h_ref, zc_ref, zp_ref, za_ref, g0_ref, g1_ref, g2_ref, wc_ref, wp_ref, wa_ref, o_ref):
    h = h_ref[0]
    m = jax.nn.sigmoid(_dot(h, g0_ref[0])) * _dot(zc_ref[0], wc_ref[0])
    m = m + jax.nn.sigmoid(_dot(h, g1_ref[0])) * _dot(zp_ref[0], wp_ref[0])
    m = m + jax.nn.sigmoid(_dot(h, g2_ref[0])) * _dot(za_ref[0], wa_ref[0])
    o_ref[0] = m.astype(BF16)


def _merge(h, zc, zp, za, w_in_all, gate_off, wc_all, wp_all, wa_all, layer):
    b, n, d = h.shape
    tm, tn = _tile(n, 512), _tile(d, 512)
    nb = d // tn
    assert gate_off % tn == 0
    g0 = gate_off // tn
    act = lambda a: pl.BlockSpec((1, tm, a.shape[-1]), lambda i, j, q: (i, j, 0))
    wcol = lambda a, off: pl.BlockSpec((1, a.shape[1], tn), lambda i, j, q: (layer, 0, q + off))
    return pl.pallas_call(
        _merge_kernel,
        grid=(b, n // tm, nb),
        in_specs=[act(h), act(zc), act(zp), act(za),
                  wcol(w_in_all, g0), wcol(w_in_all, g0 + nb), wcol(w_in_all, g0 + 2 * nb),
                  wcol(wc_all, 0), wcol(wp_all, 0), wcol(wa_all, 0)],
        out_specs=pl.BlockSpec((1, tm, tn), lambda i, j, q: (i, j, q)),
        out_shape=jax.ShapeDtypeStruct((b, n, d), BF16),
        compiler_params=_cp(("arbitrary",) * 3, 56),
        name="merge",
    )(h, zc, zp, za, w_in_all, w_in_all, w_in_all, wc_all, wp_all, wa_all)


def _out_proj_kernel(m_ref, w_ref, x_ref, mod_ref, g_ref, x1_ref, h2_ref):
    mod = mod_ref[0, 0]
    x1 = x_ref[0] + mod[2:3] * _dot(m_ref[0], w_ref[0])
    x1_ref[0] = x1
    h2_ref[0] = _modulated_norm(x1, g_ref[0], mod, 3, 4).astype(BF16)


def _out_proj(merged, wo_all, x, mod_all, gain2_all, layer, row_of):
    b, n, d = x.shape
    tm = _tile(n, 256)
    row = pl.BlockSpec((1, tm, d), lambda i, j: (i, j, 0))
    return pl.pallas_call(
        _out_proj_kernel,
        grid=(b, n // tm),
        in_specs=[row, pl.BlockSpec((1, d, d), lambda i, j: (layer, 0, 0)), row,
                  _mod_spec(mod_all, layer, row_of), _gain_spec(gain2_all, layer)],
        out_specs=[row, row],
        out_shape=[jax.ShapeDtypeStruct((b, n, d), F32), jax.ShapeDtypeStruct((b, n, d), BF16)],
        compiler_params=_cp(("arbitrary", "arbitrary")),
        name="out_proj",
    )(merged, wo_all, x, mod_all, gain2_all)


def _router_kernel(h_ref, w_ref, o_ref):
    logits = _dot_nt(w_ref[0], h_ref[0])
    e = jnp.exp(logits - jnp.max(logits, axis=0, keepdims=True))
    o_ref[0] = e / jnp.sum(e, axis=0, keepdims=True)


def _router(h2, wr_t_all, layer):
    b, n, d = h2.shape
    e = wr_t_all.shape[1]
    tn = _tile(n, 1024)
    return pl.pallas_call(
        _router_kernel,
        grid=(b, n // tn),
        in_specs=[pl.BlockSpec((1, tn, d), lambda i, j: (i, j, 0)),
                  pl.BlockSpec((1, e, d), lambda i, j: (layer, 0, 0))],
        out_specs=pl.BlockSpec((1, e, tn), lambda i, j: (i, 0, j)),
        out_shape=jax.ShapeDtypeStruct((b, e, n), F32),
        compiler_params=_cp(("arbitrary", "arbitrary")),
        name="router",
    )(h2, wr_t_all)


def _lane_cumsum(x):
    e, n = x.shape
    nk = n // LANE
    i = lax.broadcasted_iota(I32, (LANE, LANE), 0)
    j = lax.broadcasted_iota(I32, (LANE, LANE), 1)
    upper = _onehot(i <= j)
    lane = lax.broadcasted_iota(I32, (e, LANE), 1)
    off = jnp.zeros((e, 1), F32)
    bounds = jnp.zeros((e, LANE), F32)
    parts = []
    for k in range(nk):
        bounds = jnp.where(lane == k, off, bounds)
        c = _dot(x[:, k * LANE:(k + 1) * LANE].astype(BF16), upper) + off
        parts.append(c)
        off = c[:, LANE - 1:LANE]
    bounds = jnp.where(lane == nk, off, bounds)
    return jnp.concatenate(parts, axis=1), bounds


def _route_kernel(aff_ref, pos_ref, cum_ref, *, cap):
    aff = aff_ref[0]
    n_exp = aff.shape[0]
    bits = pltpu.bitcast(aff, I32)

    def search(i, prefix):
        cand = prefix | jnp.left_shift(jnp.int32(1), 30 - i)
        cnt = jnp.sum(jnp.where(bits >= cand, 1.0, 0.0), axis=1, keepdims=True)
        return jnp.where(cnt >= cap, cand, prefix)

    thr = lax.fori_loop(0, 31, search, jnp.zeros((n_exp, 1), I32))
    gt = bits > thr
    eq = bits == thr
    need = cap - jnp.sum(jnp.where(gt, 1.0, 0.0), axis=1, keepdims=True)
    eq_f = jnp.where(eq, 1.0, 0.0)
    eq_rank, _ = _lane_cumsum(eq_f)
    sel = gt | (eq & (eq_rank - eq_f < need))
    sel_f = jnp.where(sel, 1.0, 0.0)
    rank, bounds = _lane_cumsum(sel_f)
    pos_ref[0] = jnp.where(sel, (rank - sel_f).astype(I32), -1)
    cum_ref[0] = bounds.astype(I32)


def _route(aff, cap):
    b, e, n = aff.shape
    assert n % LANE == 0 and n // LANE < LANE
    return pl.pallas_call(
        functools.partial(_route_kernel, cap=cap),
        grid=(b,),
        in_specs=[pl.BlockSpec((1, e, n), lambda i: (i, 0, 0))],
        out_specs=[pl.BlockSpec((1, e, n), lambda i: (i, 0, 0)),
                   pl.BlockSpec((1, e, LANE), lambda i: (i, 0, 0))],
        out_shape=[jax.ShapeDtypeStruct((b, e, n), I32), jax.ShapeDtypeStruct((b, e, LANE), I32)],
        compiler_params=_cp(("arbitrary",)),
        name="route",
    )(aff)


def _gather_kernel(cum_ref, h_ref, pos_ref, aff_ref, xs_ref, gs_ref, *, n_exp, win, chunks_per_step):
    b, grp, k = pl.program_id(0), pl.program_id(1), pl.program_id(2)
    n_grp, capp, _ = xs_ref.shape
    tokens = h_ref.shape[1]
    e0 = grp * n_grp
    slot_iota = lax.broadcasted_iota(I32, (win, tokens), 0)

    @pl.when(k == 0)
    def _():
        xs_ref[...] = jnp.zeros_like(xs_ref)
        gs_ref[...] = jnp.zeros_like(gs_ref)

    def win_start(e):
        lo = cum_ref[(b * n_exp + e) * LANE + k * chunks_per_step]
        return pl.multiple_of(jnp.minimum((lo // BF16_ROWS) * BF16_ROWS, capp - win), BF16_ROWS)

    rows0 = pl.multiple_of(e0, n_grp)
    pos = pos_ref[0, pl.ds(rows0, n_grp), :]
    aff = aff_ref[0, pl.ds(rows0, n_grp), :]
    hits = [pos[g:g + 1, :] == slot_iota + win_start(e0 + g) for g in range(n_grp)]
    picked = _dot(jnp.concatenate([_onehot(h) for h in hits], axis=0), h_ref[0]).astype(BF16)
    for g in range(n_grp):
        ws = win_start(e0 + g)
        xs_ref[g, pl.ds(ws, win), :] += picked[g * win:(g + 1) * win]
        gs_ref[g, pl.ds(ws, win), :] += jnp.sum(jnp.where(hits[g], aff[g:g + 1, :], 0.0), axis=1, keepdims=True)

    def long_run(g, carry):
        e = e0 + g
        ws = win_start(e)
        hi = cum_ref[(b * n_exp + e) * LANE + (k + 1) * chunks_per_step]

        def extra(j, c):
            first = ws + j * win
            wj = pl.multiple_of(jnp.minimum(first, capp - win), BF16_ROWS)
            sl = slot_iota + wj
            hit = (pos_ref[0, pl.ds(e, 1), :] == sl) & (sl >= first)
            xs_ref[g, pl.ds(wj, win), :] += _dot(_onehot(hit), h_ref[0]).astype(BF16)
            gs_ref[g, pl.ds(wj, win), :] += jnp.sum(
                jnp.where(hit, aff_ref[0, pl.ds(e, 1), :], 0.0), axis=1, keepdims=True)
            return c

        lax.fori_loop(1, (hi - ws + win - 1) // win, extra, 0)
        return carry

    lax.fori_loop(0, n_grp, long_run, 0)


def _gather(h2, pos, aff, cum, capp):
    b, n, d = h2.shape
    e = pos.shape[1]
    tokens = min(GATHER_TOKENS, n)
    win = min(GATHER_WIN, capp)
    n_grp = min(EXPERT_GROUP, e)
    assert n % tokens == 0 and tokens % LANE == 0 and e % n_grp == 0
    assert capp % BF16_ROWS == 0 and win % BF16_ROWS == 0
    per_expert = pl.BlockSpec((1, e, tokens), lambda i, g, k, *_: (i, 0, k))
    grid_spec = pltpu.PrefetchScalarGridSpec(
        num_scalar_prefetch=1,
        grid=(b, e // n_grp, n // tokens),
        in_specs=[pl.BlockSpec((1, tokens, d), lambda i, g, k, *_: (i, k, 0)), per_expert, per_expert],
        out_specs=[pl.BlockSpec((n_grp, capp, d), lambda i, g, k, *_: (g, i, 0)),
                   pl.BlockSpec((n_grp, capp, 1), lambda i, g, k, *_: (g, i, 0))],
    )
    return pl.pallas_call(
        functools.partial(_gather_kernel, n_exp=e, win=win, chunks_per_step=tokens // LANE),
        grid_spec=grid_spec,
        out_shape=[jax.ShapeDtypeStruct((e, b * capp, d), BF16),
                   jax.ShapeDtypeStruct((e, b * capp, 1), F32)],
        compiler_params=_cp(("arbitrary",) * 3, 56),
        name="moe_gather",
    )(cum.reshape(-1), h2, pos, aff)


def _expert_up_kernel(*refs, n_in):
    x_refs, (wg_ref, wu_ref), o_refs = refs[:n_in], refs[n_in:n_in + 2], refs[n_in + 2:]
    wg = wg_ref[0, 0].astype(BF16)
    wu = wu_ref[0, 0].astype(BF16)
    for x_ref, o_ref in zip(x_refs, o_refs):
        x = x_ref[0]
        g = _dot(x, wg)
        o_ref[0] = (g * jax.nn.sigmoid(g) * _dot(x, wu)).astype(BF16)


def _expert_up(xs_list, w_gate_all, w_up_all, layer):
    e, _, d = xs_list[0].shape
    f = w_gate_all.shape[-1]
    tf = _tile(f, 256)
    wblk = pl.BlockSpec((1, 1, d, tf), lambda i, j: (layer, i, 0, j))
    return pl.pallas_call(
        functools.partial(_expert_up_kernel, n_in=len(xs_list)),
        grid=(e, f // tf),
        in_specs=[pl.BlockSpec((1, x.shape[1], d), lambda i, j: (i, 0, 0)) for x in xs_list] + [wblk, wblk],
        out_specs=[pl.BlockSpec((1, x.shape[1], tf), lambda i, j: (i, 0, j)) for x in xs_list],
        out_shape=[jax.ShapeDtypeStruct((e, x.shape[1], f), BF16) for x in xs_list],
        compiler_params=_cp(("arbitrary", "arbitrary"), 56),
        name="expert_up",
    )(*xs_list, w_gate_all, w_up_all)


def _expert_down_kernel(*refs, n_in):
    h_refs, g_refs, w_ref, o_refs = refs[:n_in], refs[n_in:2 * n_in], refs[2 * n_in], refs[2 * n_in + 1:]
    w = w_ref[0, 0].astype(BF16)
    for h_ref, g_ref, o_ref in zip(h_refs, g_refs, o_refs):
        o_ref[0] = (_dot(h_ref[0], w) * g_ref[0]).astype(BF16)


def _expert_down(hid_list, gs_list, w_down_all, layer):
    e, _, f = hid_list[0].shape
    d = w_down_all.shape[-1]
    tn = _tile(d, 512)
    return pl.pallas_call(
        functools.partial(_expert_down_kernel, n_in=len(hid_list)),
        grid=(e, d // tn),
        in_specs=[pl.BlockSpec((1, h.shape[1], f), lambda i, j: (i, 0, 0)) for h in hid_list]
        + [pl.BlockSpec((1, g.shape[1], 1), lambda i, j: (i, 0, 0)) for g in gs_list]
        + [pl.BlockSpec((1, 1, f, tn), lambda i, j: (layer, i, 0, j))],
        out_specs=[pl.BlockSpec((1, h.shape[1], tn), lambda i, j: (i, 0, j)) for h in hid_list],
        out_shape=[jax.ShapeDtypeStruct((e, h.shape[1], d), BF16) for h in hid_list],
        compiler_params=_cp(("arbitrary", "arbitrary"), 56),
        name="expert_down",
    )(*hid_list, *gs_list, w_down_all)


def _combine_kernel(cum_ref, pos_ref, x_ref, mod_ref, gain_ref, modn_ref, ys_hbm, *rest,
                    n_exp, capp, win, chunks_per_step, final):
    n_out = 1 if final else 2
    outs, (ybuf, xbuf, acc_ref, sem, xsem) = rest[:n_out], rest[n_out:]
    b, t = pl.program_id(0), pl.program_id(1)
    n_t = pl.num_programs(1)
    step = b * n_t + t
    n_steps = pl.num_programs(0) * n_t
    slot = lax.rem(step, 2)
    tokens = pos_ref.shape[2]
    slot_iota = lax.broadcasted_iota(I32, (win, tokens), 0)

    def win_start(bb, tt, e):
        lo = cum_ref[(bb * n_exp + e) * LANE + tt * chunks_per_step]
        return jnp.minimum((lo // BF16_ROWS) * BF16_ROWS, capp - win)

    def window_copy(bb, tt, e, s):
        src0 = pl.multiple_of(bb * capp + win_start(bb, tt, e), BF16_ROWS)
        return pltpu.make_async_copy(ys_hbm.at[e, pl.ds(src0, win), :],
                                     ybuf.at[s, pl.ds(e * win, win), :], sem.at[s])

    def start_windows(bb, tt, s):
        for e in range(n_exp):
            window_copy(bb, tt, e, s).start()

    @pl.when(step == 0)
    def _():
        start_windows(b, t, slot)

    @pl.when(step + 1 < n_steps)
    def _():
        nxt = step + 1
        start_windows(nxt // n_t, lax.rem(nxt, n_t), 1 - slot)

    onehot = jnp.concatenate(
        [_onehot(pos_ref[0, e:e + 1, :] == slot_iota + win_start(b, t, e)) for e in range(n_exp)], axis=0)
    for e in range(n_exp):
        window_copy(b, t, e, slot).wait()
    acc_ref[...] = _dot_tn(onehot, ybuf[slot])

    def long_run(e, carry):
        ws = win_start(b, t, e)
        hi = cum_ref[(b * n_exp + e) * LANE + (t + 1) * chunks_per_step]

        def extra(j, c):
            first = ws + j * win
            wj = jnp.minimum(first, capp - win)
            src0 = pl.multiple_of(b * capp + wj, BF16_ROWS)
            cp = pltpu.make_async_copy(ys_hbm.at[e, pl.ds(src0, win), :], xbuf, xsem)
            cp.start()
            cp.wait()
            sl = slot_iota + wj
            hit = (pos_ref[0, pl.ds(e, 1), :] == sl) & (sl >= first)
            acc_ref[...] += _dot_tn(_onehot(hit), xbuf[...])
            return c

        lax.fori_loop(1, (hi - ws + win - 1) // win, extra, 0)
        return carry

    lax.fori_loop(0, n_exp, long_run, 0)

    y = x_ref[0] + mod_ref[0, 0][5:6] * acc_ref[...]
    if final:
        outs[0][0] = _rms(y, gain_ref[0])
    else:
        outs[0][0] = y
        outs[1][0] = _modulated_norm(y, gain_ref[0], modn_ref[0, 0], 0, 1).astype(BF16)


def _combine(ys, pos, cum, x1, mod_all, layer, row_of, capp, gain_all, gain_layer, final):
    b, n, d = x1.shape
    e = pos.shape[1]
    tokens = min(COMBINE_TOKENS, n)
    win = min(COMBINE_WIN, capp)
    assert n % tokens == 0 and tokens % LANE == 0 and capp % BF16_ROWS == 0 and win % BF16_ROWS == 0
    mod_next_layer = layer if final else layer + 1
    row = pl.BlockSpec((1, tokens, d), lambda i, t, *_: (i, t, 0))
    grid_spec = pltpu.PrefetchScalarGridSpec(
        num_scalar_prefetch=1,
        grid=(b, n // tokens),
        in_specs=[
            pl.BlockSpec((1, e, tokens), lambda i, t, *_: (i, 0, t)),
            row,
            _mod_spec(mod_all, layer, row_of),
            _gain_spec(gain_all, gain_layer),
            _mod_spec(mod_all, mod_next_layer, row_of),
            pl.BlockSpec(memory_space=pl.ANY),
        ],
        out_specs=[row] if final else [row, row],
        scratch_shapes=[pltpu.VMEM((2, e * win, d), BF16), pltpu.VMEM((win, d), BF16),
                        pltpu.VMEM((tokens, d), F32),
                        pltpu.SemaphoreType.DMA((2,)), pltpu.SemaphoreType.DMA],
    )
    out_f32 = jax.ShapeDtypeStruct((b, n, d), F32)
    return pl.pallas_call(
        functools.partial(_combine_kernel, n_exp=e, capp=capp, win=win,
                          chunks_per_step=tokens // LANE, final=final),
        grid_spec=grid_spec,
        out_shape=[out_f32] if final else [out_f32, jax.ShapeDtypeStruct((b, n, d), BF16)],
        compiler_params=_cp(("arbitrary", "arbitrary")),
        name="moe_combine",
    )(cum.reshape(-1), pos, x1, mod_all, gain_all, mod_all, ys)


def _route_and_gather(h2, wr_t_all, layer):
    b, n, _ = h2.shape
    e = wr_t_all.shape[1]
    cap = max(1, CAPACITY_FACTOR * n // e)
    capp = -(-cap // BF16_ROWS) * BF16_ROWS
    aff = _router(h2, wr_t_all, layer)
    pos, cum = _route(aff, cap)
    xs, gs = _gather(h2, pos, aff, cum, capp)
    return dict(xs=xs, gs=gs, pos=pos, cum=cum, capp=capp)


def kernel(x, c, ctx, c_ctx, w_mod, b_mod, norm1, norm2, w_in, conv_w, pool_w, pool_scale, rpb,
           w_conv_out, w_pool_out, w_attn_out, w_o, w_router, w_e_gate, w_e_up, w_e_down, final_norm):
    depth = w_mod.shape[0]
    b, n, d = x.shape
    dc, dp = conv_w.shape[-1], pool_scale.shape[-1]
    n_heads, da = rpb.shape[1], w_attn_out.shape[1]
    off_pool, off_q = 3 * dc, 3 * dc + dp
    off_k, off_gate = off_q + da, off_q + 3 * da
    assert n % GRID_W == 0 and da % n_heads == 0

    n_rows = -(-(b + 1) // SUBLANE) * SUBLANE
    cond = jnp.zeros((n_rows, d), F32).at[:b].set(c).at[b].set(c_ctx)
    mod_all = _adaln_all(cond, w_mod, b_mod).reshape(depth, n_rows, N_MOD, d)
    lat_row, ctx_row = (lambda i: i), (lambda i: b)
    bias_all = _bias_tiles(rpb, n // GRID_W)

    w_in_b = w_in.astype(BF16)
    wc_b, wp_b, wa_b, wo_b = (w.astype(BF16) for w in (w_conv_out, w_pool_out, w_attn_out, w_o))
    wr_t = jnp.swapaxes(w_router, 1, 2).astype(BF16)
    pool_w_b = pool_w.astype(BF16)
    pool_scale_r = pool_scale.reshape(depth, pool_w.shape[1], 1, -1)
    gain1, gain2 = norm1.reshape(depth, 1, d), norm2.reshape(depth, 1, d)
    gain_f = final_norm.reshape(1, 1, d)

    def mixer(h, attn, l):
        zc = _conv_branch(h, w_in_b, conv_w, l, dc)
        zp = _pool_branch(h, w_in_b, off_pool, pool_w_b, pool_scale_r, l)
        return _merge(h, zc, zp, attn, w_in_b, off_gate, wc_b, wp_b, wa_b, l)

    hx = _norm_mod(x, gain1, mod_all, 0, lat_row)
    hc = _norm_mod(ctx, gain1, mod_all, 0, ctx_row)
    for l in range(depth):
        last = l == depth - 1
        groups = []
        if last:
            kv_c = _proj(hc, w_in_b, l, off_k, 2 * da, BF16)
            k_off, v_off = 0, n_heads
        else:
            kv_c = _proj(hc, w_in_b, l, off_q, 3 * da, BF16)
            k_off, v_off = n_heads, 2 * n_heads
            merged_c = mixer(hc, _ctx_attention(kv_c, n_heads), l)
            ctx1, hc2 = _out_proj(merged_c, wo_b, ctx, mod_all, gain2, l, ctx_row)
            groups.append(dict(_route_and_gather(hc2, wr_t, l), x1=ctx1, row_of=ctx_row))

        qkv = _proj(hx, w_in_b, l, off_q, 3 * da, BF16)
        attn = _natten(qkv, kv_c, k_off, v_off, bias_all, l, n_heads)
        x1, hx2 = _out_proj(mixer(hx, attn, l), wo_b, x, mod_all, gain2, l, lat_row)
        groups.append(dict(_route_and_gather(hx2, wr_t, l), x1=x1, row_of=lat_row))

        hid = _expert_up([g["xs"] for g in groups], w_e_gate, w_e_up, l)
        ys = _expert_down(hid, [g["gs"] for g in groups], w_e_down, l)
        res = [_combine(y, g["pos"], g["cum"], g["x1"], mod_all, l, g["row_of"], g["capp"],
                        gain_f if last else gain1, 0 if last else l + 1, last)
               for y, g in zip(ys, groups)]
        if last:
            return res[-1][0]
        (ctx, hc), (x, hx) = res
```

```python
import functools

import jax
import jax.numpy as jnp
from jax import lax
from jax.experimental import pallas as pl
from jax.experimental.pallas import tpu as pltpu

F32 = jnp.float32
BF16 = jnp.bfloat16
I32 = jnp.int32

GRID_W = 64
POOL_WINDOWS = (2, 4, 8, 16)
CAPACITY_FACTOR = 2
N_MOD = 6
EPS = 1e-6
NEG = -1e30

LANE = 128
SUBLANE = 8
BF16_ROWS = 16
MIB = 1024 * 1024
Q_ROWS = 4
HALO = 16
GATHER_TOKENS = 256
GATHER_WIN = 64
EXPERT_GROUP = 8
LOG2E = 1.4426950408889634
COMBINE_TOKENS = 256
COMBINE_WIN = 64


def _cp(sem, vmem_mib=48):
    return pltpu.CompilerParams(dimension_semantics=sem, vmem_limit_bytes=vmem_mib * MIB)


def _dot(a, b):
    return jnp.dot(a, b, preferred_element_type=F32)


def _dot_nt(a, b):
    return lax.dot_general(a, b, (((1,), (1,)), ((), ())), preferred_element_type=F32)


def _dot_tn(a, b):
    return lax.dot_general(a, b, (((0,), (0,)), ((), ())), preferred_element_type=F32)


def _tile(n, pref):
    return pref if n % pref == 0 else n


def _onehot(hit):
    return jnp.where(hit, 1.0, 0.0).astype(BF16)


def _mod_kernel(c_ref, w_ref, b_ref, o_ref):
    c = c_ref[...]
    a = (c * jax.nn.sigmoid(c)).astype(BF16)
    o_ref[0] = _dot(a, w_ref[0].astype(BF16)) + b_ref[0]


def _adaln_all(cond, w_mod, b_mod):
    n_layers, d, cols = w_mod.shape
    r = cond.shape[0]
    tn = _tile(cols, 1024)
    return pl.pallas_call(
        _mod_kernel,
        grid=(n_layers, cols // tn),
        in_specs=[
            pl.BlockSpec((r, d), lambda l, j: (0, 0)),
            pl.BlockSpec((1, d, tn), lambda l, j: (l, 0, j)),
            pl.BlockSpec((1, 1, tn), lambda l, j: (l, 0, j)),
        ],
        out_specs=pl.BlockSpec((1, r, tn), lambda l, j: (l, 0, j)),
        out_shape=jax.ShapeDtypeStruct((n_layers, r, cols), F32),
        compiler_params=_cp(("arbitrary", "arbitrary")),
        name="adaln_mod",
    )(cond, w_mod, b_mod.reshape(n_layers, 1, cols))


def _mod_spec(mod_all, layer, row_of):
    return pl.BlockSpec((1, 1) + mod_all.shape[2:], lambda i, *_: (layer, row_of(i), 0, 0))


def _gain_spec(gain_all, layer):
    return pl.BlockSpec((1, 1, gain_all.shape[-1]), lambda *_: (layer, 0, 0))


def _rms(x, g):
    return x * lax.rsqrt(jnp.mean(x * x, axis=-1, keepdims=True) + EPS) * g


def _modulated_norm(x, gain, mod, shift_i, scale_i):
    return _rms(x, gain) * (1.0 + mod[scale_i:scale_i + 1]) + mod[shift_i:shift_i + 1]


def _norm_mod_kernel(x_ref, g_ref, mod_ref, o_ref):
    o_ref[0] = _modulated_norm(x_ref[0], g_ref[0], mod_ref[0, 0], 0, 1).astype(BF16)


def _norm_mod(x, gain_all, mod_all, layer, row_of):
    b, n, d = x.shape
    tm = _tile(n, 512)
    return pl.pallas_call(
        _norm_mod_kernel,
        grid=(b, n // tm),
        in_specs=[pl.BlockSpec((1, tm, d), lambda i, j: (i, j, 0)),
                  _gain_spec(gain_all, layer), _mod_spec(mod_all, layer, row_of)],
        out_specs=pl.BlockSpec((1, tm, d), lambda i, j: (i, j, 0)),
        out_shape=jax.ShapeDtypeStruct((b, n, d), BF16),
        compiler_params=_cp(("arbitrary", "arbitrary")),
        name="norm_mod",
    )(x, gain_all, mod_all)


def _proj_kernel(h_ref, w_ref, o_ref):
    o_ref[0] = _dot(h_ref[0], w_ref[0]).astype(o_ref.dtype)


def _proj(h, w_all, layer, col_off, n_cols, out_dtype):
    b, n, k = h.shape
    tm = _tile(n, 1024)
    tn = next(t for t in (1024, 512, 256, LANE) if n_cols % t == 0 and col_off % t == 0)
    c0 = col_off // tn
    return pl.pallas_call(
        _proj_kernel,
        grid=(b, n // tm, n_cols // tn),
        in_specs=[
            pl.BlockSpec((1, tm, k), lambda i, j, q: (i, j, 0)),
            pl.BlockSpec((1, k, tn), lambda i, j, q: (layer, 0, c0 + q)),
        ],
        out_specs=pl.BlockSpec((1, tm, tn), lambda i, j, q: (i, j, q)),
        out_shape=jax.ShapeDtypeStruct((b, n, n_cols), out_dtype),
        compiler_params=_cp(("arbitrary",) * 3),
        name="in_proj",
    )(h, w_all)


def _halo_specs(n, tm, k):
    assert tm % HALO == 0 and max(POOL_WINDOWS) // 2 <= HALO // 2
    per, last = tm // HALO, n // HALO - 1
    return [pl.BlockSpec((1, HALO, k), lambda i, j, q: (i, jnp.maximum(j * per - 1, 0), 0)),
            pl.BlockSpec((1, tm, k), lambda i, j, q: (i, j, 0)),
            pl.BlockSpec((1, HALO, k), lambda i, j, q: (i, jnp.minimum((j + 1) * per, last), 0))]


def _project_with_halo(hp_ref, h_ref, hn_ref, project):
    tm = h_ref.shape[1]
    u = project(jnp.concatenate([hp_ref[0], h_ref[0], hn_ref[0]], axis=0))
    j, nj = pl.program_id(1), pl.num_programs(1)
    row = lax.broadcasted_iota(I32, u.shape, 0)
    outside = ((j == 0) & (row < HALO)) | ((j == nj - 1) & (row >= tm + HALO))
    return jnp.where(outside, 0.0, u)


def _conv_branch_kernel(hp_ref, h_ref, hn_ref, wb_ref, wc_ref, wx_ref, cw_ref, o_ref):
    tm = h_ref.shape[1]
    u = _project_with_halo(hp_ref, h_ref, hn_ref, lambda a: _dot(a, wc_ref[0]) * _dot(a, wx_ref[0]))
    rows = u.shape[0]
    centre = slice(HALO, HALO + tm)
    w = cw_ref[0]
    conv = (pltpu.roll(u, 1, 0)[centre] * w[0:1] + u[centre] * w[1:2]
            + pltpu.roll(u, rows - 1, 0)[centre] * w[2:3])
    o_ref[0] = (_dot(h_ref[0], wb_ref[0]) * conv).astype(BF16)


def _conv_branch(h, w_all, conv_w_all, layer, dc):
    b, n, k = h.shape
    tm, tn = _tile(n, 1024), _tile(dc, 512)
    nb = dc // tn
    wblk = lambda off: pl.BlockSpec((1, k, tn), lambda i, j, q: (layer, 0, q + off))
    return pl.pallas_call(
        _conv_branch_kernel,
        grid=(b, n // tm, nb),
        in_specs=_halo_specs(n, tm, k) + [wblk(0), wblk(nb), wblk(2 * nb),
                                          pl.BlockSpec((1, conv_w_all.shape[1], tn), lambda i, j, q: (layer, 0, q))],
        out_specs=pl.BlockSpec((1, tm, tn), lambda i, j, q: (i, j, q)),
        out_shape=jax.ShapeDtypeStruct((b, n, dc), BF16),
        compiler_params=_cp(("arbitrary",) * 3, 56),
        name="conv_branch",
    )(h, h, h, w_all, w_all, w_all, conv_w_all)


def _pool_branch_kernel(hp_ref, h_ref, hn_ref, wi_ref, pw_ref, s_ref, o_ref, *, n):
    grp = pl.program_id(2)
    tm = h_ref.shape[1]
    u = _project_with_halo(hp_ref, h_ref, hn_ref, lambda a: _dot(a, wi_ref[0]))
    rows = u.shape[0]
    centre = slice(HALO, HALO + tm)
    t = lax.broadcasted_iota(I32, (tm, u.shape[1]), 0) + pl.program_id(1) * tm

    def shifted(x, s):
        return pltpu.roll(x, s % rows, 0)

    for gi, win in enumerate(POOL_WINDOWS):
        @pl.when(grp == gi)
        def _(win=win):
            w = shifted(u, 1) + u
            span = 2
            while span < win:
                half = span // 2
                w = shifted(w, half) + shifted(w, -half)
                span *= 2
            cnt = jnp.minimum(t + win // 2, n) - jnp.maximum(t - win // 2, 0)
            mixed = w[centre] / cnt.astype(F32) - u[centre]
            y = _dot(mixed.astype(BF16), pw_ref[0, 0]) * s_ref[0, 0]
            o_ref[0] = y.astype(BF16)


def _pool_branch(h, w_all, col_off, pool_w_all, pool_scale_all, layer):
    b, n, k = h.shape
    g, pg = pool_w_all.shape[1], pool_w_all.shape[2]
    assert g == len(POOL_WINDOWS) and col_off % pg == 0
    c0 = col_off // pg
    tm = _tile(n, 1024)
    return pl.pallas_call(
        functools.partial(_pool_branch_kernel, n=n),
        grid=(b, n // tm, g),
        in_specs=_halo_specs(n, tm, k) + [
            pl.BlockSpec((1, k, pg), lambda i, j, q: (layer, 0, c0 + q)),
            pl.BlockSpec((1, 1, pg, pg), lambda i, j, q: (layer, q, 0, 0)),
            pl.BlockSpec((1, 1, 1, pg), lambda i, j, q: (layer, q, 0, 0)),
        ],
        out_specs=pl.BlockSpec((1, tm, pg), lambda i, j, q: (i, j, q)),
        out_shape=jax.ShapeDtypeStruct((b, n, g * pg), BF16),
        compiler_params=_cp(("arbitrary",) * 3),
        name="pool_branch",
    )(h, h, h, w_all, pool_w_all, pool_scale_all)


def _group_geometry(rows, win_rows):
    key_rows = Q_ROWS + win_rows
    n_groups = rows // Q_ROWS
    assert rows % Q_ROWS == 0 and n_groups >= 3 and rows >= key_rows
    return key_rows, n_groups, (0, 1, n_groups - 1)


def _bias_kernel(rpb_ref, o_ref, *, rows, win_rows, win_cols):
    key_rows, _, reps = _group_geometry(rows, win_rows)
    w = GRID_W
    assert 2 * w == LANE
    qc = lax.broadcasted_iota(I32, (w, LANE), 0)
    lane = lax.broadcasted_iota(I32, (w, LANE), 1)
    first = lane < w
    kc = jnp.where(first, lane, lane - w)
    cstart = jnp.clip(qc - win_cols // 2, 0, w - win_cols)
    in_cols = (kc >= cstart) & (kc < cstart + win_cols)
    first8 = first[:SUBLANE]

    def bias_row(dr):
        if dr is None:
            return jnp.zeros((SUBLANE, LANE), F32)
        return jnp.broadcast_to(rpb_ref[0, 0, dr:dr + 1, :], (SUBLANE, LANE))

    def pair_block(dr_a, dr_b):
        if dr_a is None and dr_b is None:
            return jnp.full((w, LANE), NEG, F32)
        r = jnp.where(first8, bias_row(dr_a), pltpu.roll(bias_row(dr_b), w, 1))
        t = pltpu.roll(jnp.broadcast_to(r[0:1], (w, LANE)), LANE - (win_cols - 1), 1, stride=1, stride_axis=0)
        ok = in_cols if dr_a is not None and dr_b is not None else in_cols & (first if dr_b is None else ~first)
        return jnp.where(ok, t, NEG) * LOG2E

    for ti, g in enumerate(reps):
        kb = min(max(Q_ROWS * g - win_rows // 2, 0), rows - key_rows)
        for i in range(Q_ROWS):
            r = Q_ROWS * g + i
            rs = min(max(r - win_rows // 2, 0), rows - win_rows)
            drs = []
            for j in range(key_rows):
                kr = kb + j
                drs.append(kr - r + win_rows - 1 if rs <= kr < rs + win_rows else None)
            for jp in range(key_rows // 2):
                o_ref[0, 0, ti, i * w:(i + 1) * w, jp * 2 * w:(jp + 1) * 2 * w] = pair_block(
                    drs[2 * jp], drs[2 * jp + 1])


def _bias_tiles(rpb, rows):
    n_layers, n_heads, n_dr, n_dc = rpb.shape
    win_rows, win_cols = (n_dr + 1) // 2, (n_dc + 1) // 2
    key_rows, _, _ = _group_geometry(rows, win_rows)
    assert key_rows % 2 == 0
    tq, tk = Q_ROWS * GRID_W, key_rows * GRID_W
    rpb_lanes = jnp.pad(rpb, ((0, 0), (0, 0), (0, 0), (0, LANE - n_dc)))
    return pl.pallas_call(
        functools.partial(_bias_kernel, rows=rows, win_rows=win_rows, win_cols=win_cols),
        grid=(n_layers, n_heads),
        in_specs=[pl.BlockSpec((1, 1, n_dr, LANE), lambda l, h: (l, h, 0, 0))],
        out_specs=pl.BlockSpec((1, 1, 3, tq, tk), lambda l, h: (l, h, 0, 0, 0)),
        out_shape=jax.ShapeDtypeStruct((n_layers, n_heads, 3, tq, tk), F32),
        compiler_params=_cp(("arbitrary", "arbitrary")),
        name="bias_tiles",
    )(rpb_lanes)


def _natten_kernel(q_ref, k_ref, v_ref, kc_ref, vc_ref, bias_ref, o_ref, *, rows, win_rows):
    key_rows, n_groups, _ = _group_geometry(rows, win_rows)
    tq, tk = Q_ROWS * GRID_W, key_rows * GRID_W
    scale = q_ref.shape[-1] ** -0.5 * LOG2E
    kc = kc_ref[0]
    vc = vc_ref[0]

    def group(g, carry):
        kb = jnp.clip(Q_ROWS * g - win_rows // 2, 0, rows - key_rows)
        kind = jnp.where(g == 0, 0, jnp.where(g == n_groups - 1, 2, 1))
        q0 = pl.multiple_of(g * tq, tq)
        k0 = pl.multiple_of(kb * GRID_W, GRID_W)
        q = q_ref[0, pl.ds(q0, tq), :]
        kw = k_ref[0, pl.ds(k0, tk), :]
        vw = v_ref[0, pl.ds(k0, tk), :]
        s_win = _dot_nt(q, kw) * scale + bias_ref[0, 0, kind]
        s_ctx = _dot_nt(q, kc) * scale
        m = jnp.maximum(jnp.max(s_win, axis=-1, keepdims=True), jnp.max(s_ctx, axis=-1, keepdims=True))
        p_win = jnp.exp2(s_win - m)
        p_ctx = jnp.exp2(s_ctx - m)
        inv = 1.0 / (jnp.sum(p_win, axis=-1, keepdims=True) + jnp.sum(p_ctx, axis=-1, keepdims=True))
        o = (_dot(p_win.astype(BF16), vw) + _dot(p_ctx.astype(BF16), vc)) * inv
        o_ref[0, pl.ds(q0, tq), :] = o.astype(BF16)
        return carry

    lax.fori_loop(0, n_groups, group, 0, unroll=2)


def _natten(qkv, kv_ctx, k_off, v_off, bias_all, layer, n_heads):
    b, n, c3 = qkv.shape
    dh = c3 // (3 * n_heads)
    lc = kv_ctx.shape[1]
    rows = n // GRID_W
    win_rows = bias_all.shape[-1] // GRID_W - Q_ROWS
    seq = lambda off: pl.BlockSpec((1, n, dh), lambda i, h: (i, 0, off + h))
    ctx = lambda off: pl.BlockSpec((1, lc, dh), lambda i, h: (i, 0, off + h))
    return pl.pallas_call(
        functools.partial(_natten_kernel, rows=rows, win_rows=win_rows),
        grid=(b, n_heads),
        in_specs=[
            seq(0), seq(n_heads), seq(2 * n_heads), ctx(k_off), ctx(v_off),
            pl.BlockSpec((1, 1) + bias_all.shape[2:], lambda i, h: (layer, h, 0, 0, 0)),
        ],
        out_specs=pl.BlockSpec((1, n, dh), lambda i, h: (i, 0, h)),
        out_shape=jax.ShapeDtypeStruct((b, n, n_heads * dh), BF16),
        compiler_params=_cp(("arbitrary", "arbitrary")),
        name="natten",
    )(qkv, qkv, qkv, kv_ctx, kv_ctx, bias_all)


def _ctx_attn_kernel(q_ref, k_ref, v_ref, o_ref):
    q = q_ref[0]
    s = _dot_nt(q, k_ref[0]) * (q.shape[-1] ** -0.5)
    p = jnp.exp(s - jnp.max(s, axis=-1, keepdims=True))
    p = p * (1.0 / jnp.sum(p, axis=-1, keepdims=True))
    o_ref[0] = _dot(p.astype(BF16), v_ref[0]).astype(BF16)


def _ctx_attention(qkv, n_heads):
    b, lc, c3 = qkv.shape
    dh = c3 // (3 * n_heads)
    blk = lambda off: pl.BlockSpec((1, lc, dh), lambda i, h: (i, 0, off + h))
    return pl.pallas_call(
        _ctx_attn_kernel,
        grid=(b, n_heads),
        in_specs=[blk(0), blk(n_heads), blk(2 * n_heads)],
        out_specs=blk(0),
        out_shape=jax.ShapeDtypeStruct((b, lc, n_heads * dh), BF16),
        compiler_params=_cp(("arbitrary", "arbitrary")),
        name="ctx_attention",
    )(qkv, qkv, qkv)


def _merge_kernel(h_ref, zc_ref, zp_ref, za_ref, g0_ref, g1_ref, g2_ref, wc_ref, wp_ref, wa_ref, o_ref):
    h = h_ref[0]
    m = jax.nn.sigmoid(_dot(h, g0_ref[0])) * _dot(zc_ref[0], wc_ref[0])
    m = m + jax.nn.sigmoid(_dot(h, g1_ref[0])) * _dot(zp_ref[0], wp_ref[0])
    m = m + jax.nn.sigmoid(_dot(h, g2_ref[0])) * _dot(za_ref[0], wa_ref[0])
    o_ref[0] = m.astype(BF16)


def _merge(h, zc, zp, za, w_in_all, gate_off, wc_all, wp_all, wa_all, layer):
    b, n, d = h.shape
    tm, tn = _tile(n, 512), _tile(d, 512)
    nb = d // tn
    assert gate_off % tn == 0
    g0 = gate_off // tn
    act = lambda a: pl.BlockSpec((1, tm, a.shape[-1]), lambda i, j, q: (i, j, 0))
    wcol = lambda a, off: pl.BlockSpec((1, a.shape[1], tn), lambda i, j, q: (layer, 0, q + off))
    return pl.pallas_call(
        _merge_kernel,
        grid=(b, n // tm, nb),
        in_specs=[act(h), act(zc), act(zp), act(za),
                  wcol(w_in_all, g0), wcol(w_in_all, g0 + nb), wcol(w_in_all, g0 + 2 * nb),
                  wcol(wc_all, 0), wcol(wp_all, 0), wcol(wa_all, 0)],
        out_specs=pl.BlockSpec((1, tm, tn), lambda i, j, q: (i, j, q)),
        out_shape=jax.ShapeDtypeStruct((b, n, d), BF16),
        compiler_params=_cp(("arbitrary",) * 3, 56),
        name="merge",
    )(h, zc, zp, za, w_in_all, w_in_all, w_in_all, wc_all, wp_all, wa_all)


def _out_proj_kernel(m_ref, w_ref, x_ref, mod_ref, g_ref, x1_ref, h2_ref):
    mod = mod_ref[0, 0]
    x1 = x_ref[0] + mod[2:3] * _dot(m_ref[0], w_ref[0])
    x1_ref[0] = x1
    h2_ref[0] = _modulated_norm(x1, g_ref[0], mod, 3, 4).astype(BF16)


def _out_proj(merged, wo_all, x, mod_all, gain2_all, layer, row_of):
    b, n, d = x.shape
    tm = _tile(n, 256)
    row = pl.BlockSpec((1, tm, d), lambda i, j: (i, j, 0))
    return pl.pallas_call(
        _out_proj_kernel,
        grid=(b, n // tm),
        in_specs=[row, pl.BlockSpec((1, d, d), lambda i, j: (layer, 0, 0)), row,
                  _mod_spec(mod_all, layer, row_of), _gain_spec(gain2_all, layer)],
        out_specs=[row, row],
        out_shape=[jax.ShapeDtypeStruct((b, n, d), F32), jax.ShapeDtypeStruct((b, n, d), BF16)],
        compiler_params=_cp(("arbitrary", "arbitrary")),
        name="out_proj",
    )(merged, wo_all, x, mod_all, gain2_all)


def _router_kernel(h_ref, w_ref, o_ref):
    logits = _dot_nt(w_ref[0], h_ref[0])
    e = jnp.exp(logits - jnp.max(logits, axis=0, keepdims=True))
    o_ref[0] = e / jnp.sum(e, axis=0, keepdims=True)


def _router(h2, wr_t_all, layer):
    b, n, d = h2.shape
    e = wr_t_all.shape[1]
    tn = _tile(n, 1024)
    return pl.pallas_call(
        _router_kernel,
        grid=(b, n // tn),
        in_specs=[pl.BlockSpec((1, tn, d), lambda i, j: (i, j, 0)),
                  pl.BlockSpec((1, e, d), lambda i, j: (layer, 0, 0))],
        out_specs=pl.BlockSpec((1, e, tn), lambda i, j: (i, 0, j)),
        out_shape=jax.ShapeDtypeStruct((b, e, n), F32),
        compiler_params=_cp(("arbitrary", "arbitrary")),
        name="router",
    )(h2, wr_t_all)


def _lane_cumsum(x):
    e, n = x.shape
    nk = n // LANE
    i = lax.broadcasted_iota(I32, (LANE, LANE), 0)
    j = lax.broadcasted_iota(I32, (LANE, LANE), 1)
    upper = _onehot(i <= j)
    lane = lax.broadcasted_iota(I32, (e, LANE), 1)
    off = jnp.zeros((e, 1), F32)
    bounds = jnp.zeros((e, LANE), F32)
    parts = []
    for k in range(nk):
        bounds = jnp.where(lane == k, off, bounds)
        c = _dot(x[:, k * LANE:(k + 1) * LANE].astype(BF16), upper) + off
        parts.append(c)
        off = c[:, LANE - 1:LANE]
    bounds = jnp.where(lane == nk, off, bounds)
    return jnp.concatenate(parts, axis=1), bounds


def _route_kernel(aff_ref, pos_ref, cum_ref, *, cap):
    aff = aff_ref[0]
    n_exp = aff.shape[0]
    bits = pltpu.bitcast(aff, I32)

    def search(i, prefix):
        cand = prefix | jnp.left_shift(jnp.int32(1), 30 - i)
        cnt = jnp.sum(jnp.where(bits >= cand, 1.0, 0.0), axis=1, keepdims=True)
        return jnp.where(cnt >= cap, cand, prefix)

    thr = lax.fori_loop(0, 31, search, jnp.zeros((n_exp, 1), I32))
    gt = bits > thr
    eq = bits == thr
    need = cap - jnp.sum(jnp.where(gt, 1.0, 0.0), axis=1, keepdims=True)
    eq_f = jnp.where(eq, 1.0, 0.0)
    eq_rank, _ = _lane_cumsum(eq_f)
    sel = gt | (eq & (eq_rank - eq_f < need))
    sel_f = jnp.where(sel, 1.0, 0.0)
    rank, bounds = _lane_cumsum(sel_f)
    pos_ref[0] = jnp.where(sel, (rank - sel_f).astype(I32), -1)
    cum_ref[0] = bounds.astype(I32)


def _route(aff, cap):
    b, e, n = aff.shape
    assert n % LANE == 0 and n // LANE < LANE
    return pl.pallas_call(
        functools.partial(_route_kernel, cap=cap),
        grid=(b,),
        in_specs=[pl.BlockSpec((1, e, n), lambda i: (i, 0, 0))],
        out_specs=[pl.BlockSpec((1, e, n), lambda i: (i, 0, 0)),
                   pl.BlockSpec((1, e, LANE), lambda i: (i, 0, 0))],
        out_shape=[jax.ShapeDtypeStruct((b, e, n), I32), jax.ShapeDtypeStruct((b, e, LANE), I32)],
        compiler_params=_cp(("arbitrary",)),
        name="route",
    )(aff)


def _gather_kernel(cum_ref, h_ref, pos_ref, aff_ref, xs_ref, gs_ref, *, n_exp, win, chunks_per_step):
    b, grp, k = pl.program_id(0), pl.program_id(1), pl.program_id(2)
    n_grp, capp, _ = xs_ref.shape
    tokens = h_ref.shape[1]
    e0 = grp * n_grp
    slot_iota = lax.broadcasted_iota(I32, (win, tokens), 0)

    @pl.when(k == 0)
    def _():
        xs_ref[...] = jnp.zeros_like(xs_ref)
        gs_ref[...] = jnp.zeros_like(gs_ref)

    def win_start(e):
        lo = cum_ref[(b * n_exp + e) * LANE + k * chunks_per_step]
        return pl.multiple_of(jnp.minimum((lo // BF16_ROWS) * BF16_ROWS, capp - win), BF16_ROWS)

    rows0 = pl.multiple_of(e0, n_grp)
    pos = pos_ref[0, pl.ds(rows0, n_grp), :]
    aff = aff_ref[0, pl.ds(rows0, n_grp), :]
    hits = [pos[g:g + 1, :] == slot_iota + win_start(e0 + g) for g in range(n_grp)]
    picked = _dot(jnp.concatenate([_onehot(h) for h in hits], axis=0), h_ref[0]).astype(BF16)
    for g in range(n_grp):
        ws = win_start(e0 + g)
        xs_ref[g, pl.ds(ws, win), :] += picked[g * win:(g + 1) * win]
        gs_ref[g, pl.ds(ws, win), :] += jnp.sum(jnp.where(hits[g], aff[g:g + 1, :], 0.0), axis=1, keepdims=True)

    def long_run(g, carry):
        e = e0 + g
        ws = win_start(e)
        hi = cum_ref[(b * n_exp + e) * LANE + (k + 1) * chunks_per_step]

        def extra(j, c):
            first = ws + j * win
            wj = pl.multiple_of(jnp.minimum(first, capp - win), BF16_ROWS)
            sl = slot_iota + wj
            hit = (pos_ref[0, pl.ds(e, 1), :] == sl) & (sl >= first)
            xs_ref[g, pl.ds(wj, win), :] += _dot(_onehot(hit), h_ref[0]).astype(BF16)
            gs_ref[g, pl.ds(wj, win), :] += jnp.sum(
                jnp.where(hit, aff_ref[0, pl.ds(e, 1), :], 0.0), axis=1, keepdims=True)
            return c

        lax.fori_loop(1, (hi - ws + win - 1) // win, extra, 0)
        return carry

    longest = functools.reduce(jnp.maximum, [
        cum_ref[(b * n_exp + e0 + g) * LANE + (k + 1) * chunks_per_step] - win_start(e0 + g) for g in range(n_grp)])

    @pl.when(longest > win)
    def _():
        lax.fori_loop(0, n_grp, long_run, 0)


def _gather(h2, pos, aff, cum, capp):
    b, n, d = h2.shape
    e = pos.shape[1]
    tokens = min(GATHER_TOKENS, n)
    win = min(GATHER_WIN, capp)
    n_grp = min(EXPERT_GROUP, e)
    assert n % tokens == 0 and tokens % LANE == 0 and e % n_grp == 0
    assert capp % BF16_ROWS == 0 and win % BF16_ROWS == 0
    per_expert = pl.BlockSpec((1, e, tokens), lambda i, g, k, *_: (i, 0, k))
    grid_spec = pltpu.PrefetchScalarGridSpec(
        num_scalar_prefetch=1,
        grid=(b, e // n_grp, n // tokens),
        in_specs=[pl.BlockSpec((1, tokens, d), lambda i, g, k, *_: (i, k, 0)), per_expert, per_expert],
        out_specs=[pl.BlockSpec((n_grp, capp, d), lambda i, g, k, *_: (g, i, 0)),
                   pl.BlockSpec((n_grp, capp, 1), lambda i, g, k, *_: (g, i, 0))],
    )
    return pl.pallas_call(
        functools.partial(_gather_kernel, n_exp=e, win=win, chunks_per_step=tokens // LANE),
        grid_spec=grid_spec,
        out_shape=[jax.ShapeDtypeStruct((e, b * capp, d), BF16),
                   jax.ShapeDtypeStruct((e, b * capp, 1), F32)],
        compiler_params=_cp(("arbitrary",) * 3, 56),
        name="moe_gather",
    )(cum.reshape(-1), h2, pos, aff)


def _expert_up_kernel(*refs, n_in):
    x_refs, (wg_ref, wu_ref), o_refs = refs[:n_in], refs[n_in:n_in + 2], refs[n_in + 2:]
    wg = wg_ref[0, 0].astype(BF16)
    wu = wu_ref[0, 0].astype(BF16)
    for x_ref, o_ref in zip(x_refs, o_refs):
        x = x_ref[0]
        g = _dot(x, wg)
        o_ref[0] = (g * jax.nn.sigmoid(g) * _dot(x, wu)).astype(BF16)


def _expert_up(xs_list, w_gate_all, w_up_all, layer):
    e, _, d = xs_list[0].shape
    f = w_gate_all.shape[-1]
    tf = _tile(f, 512)
    wblk = pl.BlockSpec((1, 1, d, tf), lambda i, j: (layer, i, 0, j))
    return pl.pallas_call(
        functools.partial(_expert_up_kernel, n_in=len(xs_list)),
        grid=(e, f // tf),
        in_specs=[pl.BlockSpec((1, x.shape[1], d), lambda i, j: (i, 0, 0)) for x in xs_list] + [wblk, wblk],
        out_specs=[pl.BlockSpec((1, x.shape[1], tf), lambda i, j: (i, 0, j)) for x in xs_list],
        out_shape=[jax.ShapeDtypeStruct((e, x.shape[1], f), BF16) for x in xs_list],
        compiler_params=_cp(("arbitrary", "arbitrary"), 56),
        name="expert_up",
    )(*xs_list, w_gate_all, w_up_all)


def _expert_down_kernel(*refs, n_in):
    h_refs, g_refs, w_ref, o_refs = refs[:n_in], refs[n_in:2 * n_in], refs[2 * n_in], refs[2 * n_in + 1:]
    w = w_ref[0, 0].astype(BF16)
    for h_ref, g_ref, o_ref in zip(h_refs, g_refs, o_refs):
        o_ref[0] = (_dot(h_ref[0], w) * g_ref[0]).astype(BF16)


def _expert_down(hid_list, gs_list, w_down_all, layer):
    e, _, f = hid_list[0].shape
    d = w_down_all.shape[-1]
    tn = _tile(d, 512)
    return pl.pallas_call(
        functools.partial(_expert_down_kernel, n_in=len(hid_list)),
        grid=(e, d // tn),
        in_specs=[pl.BlockSpec((1, h.shape[1], f), lambda i, j: (i, 0, 0)) for h in hid_list]
        + [pl.BlockSpec((1, g.shape[1], 1), lambda i, j: (i, 0, 0)) for g in gs_list]
        + [pl.BlockSpec((1, 1, f, tn), lambda i, j: (layer, i, 0, j))],
        out_specs=[pl.BlockSpec((1, h.shape[1], tn), lambda i, j: (i, 0, j)) for h in hid_list],
        out_shape=[jax.ShapeDtypeStruct((e, h.shape[1], d), BF16) for h in hid_list],
        compiler_params=_cp(("arbitrary", "arbitrary"), 56),
        name="expert_down",
    )(*hid_list, *gs_list, w_down_all)


def _combine_kernel(cum_ref, pos_ref, x_ref, mod_ref, gain_ref, modn_ref, ys_hbm, *rest,
                    n_exp, capp, win, chunks_per_step, final):
    n_out = 1 if final else 2
    outs, (ybuf, xbuf, acc_ref, sem, xsem) = rest[:n_out], rest[n_out:]
    b, t = pl.program_id(0), pl.program_id(1)
    n_t = pl.num_programs(1)
    step = b * n_t + t
    n_steps = pl.num_programs(0) * n_t
    slot = lax.rem(step, 2)
    tokens = pos_ref.shape[2]
    slot_iota = lax.broadcasted_iota(I32, (win, tokens), 0)

    def win_start(bb, tt, e):
        lo = cum_ref[(bb * n_exp + e) * LANE + tt * chunks_per_step]
        return jnp.minimum((lo // BF16_ROWS) * BF16_ROWS, capp - win)

    def window_copy(bb, tt, e, s):
        src0 = pl.multiple_of(bb * capp + win_start(bb, tt, e), BF16_ROWS)
        return pltpu.make_async_copy(ys_hbm.at[e, pl.ds(src0, win), :],
                                     ybuf.at[s, pl.ds(e * win, win), :], sem.at[s])

    def start_windows(bb, tt, s):
        for e in range(n_exp):
            window_copy(bb, tt, e, s).start()

    @pl.when(step == 0)
    def _():
        start_windows(b, t, slot)

    @pl.when(step + 1 < n_steps)
    def _():
        nxt = step + 1
        start_windows(nxt // n_t, lax.rem(nxt, n_t), 1 - slot)

    onehot = jnp.concatenate(
        [_onehot(pos_ref[0, e:e + 1, :] == slot_iota + win_start(b, t, e)) for e in range(n_exp)], axis=0)
    for e in range(n_exp):
        window_copy(b, t, e, slot).wait()
    acc_ref[...] = _dot_tn(onehot, ybuf[slot])

    def long_run(e, carry):
        ws = win_start(b, t, e)
        hi = cum_ref[(b * n_exp + e) * LANE + (t + 1) * chunks_per_step]

        def extra(j, c):
            first = ws + j * win
            wj = jnp.minimum(first, capp - win)
            src0 = pl.multiple_of(b * capp + wj, BF16_ROWS)
            cp = pltpu.make_async_copy(ys_hbm.at[e, pl.ds(src0, win), :], xbuf, xsem)
            cp.start()
            cp.wait()
            sl = slot_iota + wj
            hit = (pos_ref[0, pl.ds(e, 1), :] == sl) & (sl >= first)
            acc_ref[...] += _dot_tn(_onehot(hit), xbuf[...])
            return c

        lax.fori_loop(1, (hi - ws + win - 1) // win, extra, 0)
        return carry

    longest = functools.reduce(jnp.maximum, [
        cum_ref[(b * n_exp + e) * LANE + (t + 1) * chunks_per_step] - win_start(b, t, e) for e in range(n_exp)])

    @pl.when(longest > win)
    def _():
        lax.fori_loop(0, n_exp, long_run, 0)

    y = x_ref[0] + mod_ref[0, 0][5:6] * acc_ref[...]
    if final:
        outs[0][0] = _rms(y, gain_ref[0])
    else:
        outs[0][0] = y
        outs[1][0] = _modulated_norm(y, gain_ref[0], modn_ref[0, 0], 0, 1).astype(BF16)


def _combine(ys, pos, cum, x1, mod_all, layer, row_of, capp, gain_all, gain_layer, final):
    b, n, d = x1.shape
    e = pos.shape[1]
    tokens = min(COMBINE_TOKENS, n)
    win = min(COMBINE_WIN, capp)
    assert n % tokens == 0 and tokens % LANE == 0 and capp % BF16_ROWS == 0 and win % BF16_ROWS == 0
    mod_next_layer = layer if final else layer + 1
    row = pl.BlockSpec((1, tokens, d), lambda i, t, *_: (i, t, 0))
    grid_spec = pltpu.PrefetchScalarGridSpec(
        num_scalar_prefetch=1,
        grid=(b, n // tokens),
        in_specs=[
            pl.BlockSpec((1, e, tokens), lambda i, t, *_: (i, 0, t)),
            row,
            _mod_spec(mod_all, layer, row_of),
            _gain_spec(gain_all, gain_layer),
            _mod_spec(mod_all, mod_next_layer, row_of),
            pl.BlockSpec(memory_space=pl.ANY),
        ],
        out_specs=[row] if final else [row, row],
        scratch_shapes=[pltpu.VMEM((2, e * win, d), BF16), pltpu.VMEM((win, d), BF16),
                        pltpu.VMEM((tokens, d), F32),
                        pltpu.SemaphoreType.DMA((2,)), pltpu.SemaphoreType.DMA],
    )
    out_f32 = jax.ShapeDtypeStruct((b, n, d), F32)
    return pl.pallas_call(
        functools.partial(_combine_kernel, n_exp=e, capp=capp, win=win,
                          chunks_per_step=tokens // LANE, final=final),
        grid_spec=grid_spec,
        out_shape=[out_f32] if final else [out_f32, jax.ShapeDtypeStruct((b, n, d), BF16)],
        compiler_params=_cp(("arbitrary", "arbitrary")),
        name="moe_combine",
    )(cum.reshape(-1), pos, x1, mod_all, gain_all, mod_all, ys)


def _route_and_gather(h2, wr_t_all, layer):
    b, n, _ = h2.shape
    e = wr_t_all.shape[1]
    cap = max(1, CAPACITY_FACTOR * n // e)
    capp = -(-cap // BF16_ROWS) * BF16_ROWS
    aff = _router(h2, wr_t_all, layer)
    pos, cum = _route(aff, cap)
    xs, gs = _gather(h2, pos, aff, cum, capp)
    return dict(xs=xs, gs=gs, pos=pos, cum=cum, capp=capp)


def kernel(x, c, ctx, c_ctx, w_mod, b_mod, norm1, norm2, w_in, conv_w, pool_w, pool_scale, rpb,
           w_conv_out, w_pool_out, w_attn_out, w_o, w_router, w_e_gate, w_e_up, w_e_down, final_norm):
    depth = w_mod.shape[0]
    b, n, d = x.shape
    dc, dp = conv_w.shape[-1], pool_scale.shape[-1]
    n_heads, da = rpb.shape[1], w_attn_out.shape[1]
    off_pool, off_q = 3 * dc, 3 * dc + dp
    off_k, off_gate = off_q + da, off_q + 3 * da
    assert n % GRID_W == 0 and da % n_heads == 0

    n_rows = -(-(b + 1) // SUBLANE) * SUBLANE
    cond = jnp.zeros((n_rows, d), F32).at[:b].set(c).at[b].set(c_ctx)
    mod_all = _adaln_all(cond, w_mod, b_mod).reshape(depth, n_rows, N_MOD, d)
    lat_row, ctx_row = (lambda i: i), (lambda i: b)
    bias_all = _bias_tiles(rpb, n // GRID_W)

    w_in_b = w_in.astype(BF16)
    wc_b, wp_b, wa_b, wo_b = (w.astype(BF16) for w in (w_conv_out, w_pool_out, w_attn_out, w_o))
    wr_t = jnp.swapaxes(w_router, 1, 2).astype(BF16)
    pool_w_b = pool_w.astype(BF16)
    pool_scale_r = pool_scale.reshape(depth, pool_w.shape[1], 1, -1)
    gain1, gain2 = norm1.reshape(depth, 1, d), norm2.reshape(depth, 1, d)
    gain_f = final_norm.reshape(1, 1, d)

    def mixer(h, attn, l, as_rows):
        zc = _conv_branch(h, w_in_b, conv_w, l, dc)
        zp = _pool_branch(h, w_in_b, off_pool, pool_w_b, pool_scale_r, l)
        return _merge(as_rows(h), as_rows(zc), as_rows(zp), as_rows(attn), w_in_b, off_gate, wc_b, wp_b, wa_b, l)

    joined = lambda a: a.reshape(1, -1, a.shape[-1])
    per_sample = lambda a: a.reshape(b, -1, a.shape[-1])
    keep = lambda a: a

    hx = _norm_mod(x, gain1, mod_all, 0, lat_row)
    hc = _norm_mod(ctx, gain1, mod_all, 0, ctx_row)
    for l in range(depth):
        last = l == depth - 1
        groups = []
        if last:
            kv_c = per_sample(_proj(joined(hc), w_in_b, l, off_k, 2 * da, BF16))
            k_off, v_off = 0, n_heads
        else:
            kv_c = per_sample(_proj(joined(hc), w_in_b, l, off_q, 3 * da, BF16))
            k_off, v_off = n_heads, 2 * n_heads
            merged_c = mixer(hc, _ctx_attention(kv_c, n_heads), l, joined)
            ctx1, hc2 = _out_proj(merged_c, wo_b, joined(ctx), mod_all, gain2, l, ctx_row)
            ctx1, hc2 = per_sample(ctx1), per_sample(hc2)
            groups.append(dict(_route_and_gather(hc2, wr_t, l), x1=ctx1, row_of=ctx_row))

        qkv = _proj(hx, w_in_b, l, off_q, 3 * da, BF16)
        attn = _natten(qkv, kv_c, k_off, v_off, bias_all, l, n_heads)
        x1, hx2 = _out_proj(mixer(hx, attn, l, keep), wo_b, x, mod_all, gain2, l, lat_row)
        groups.append(dict(_route_and_gather(hx2, wr_t, l), x1=x1, row_of=lat_row))

        hid = _expert_up([g["xs"] for g in groups], w_e_gate, w_e_up, l)
        ys = _expert_down(hid, [g["gs"] for g in groups], w_e_down, l)
        res = [_combine(y, g["pos"], g["cum"], g["x1"], mod_all, l, g["row_of"], g["capp"],
                        gain_f if last else gain1, 0 if last else l + 1, last)
               for y, g in zip(ys, groups)]
        if last:
            return res[-1][0]
        (ctx, hc), (x, hx) = res
```

```python
import functools

import jax
import jax.numpy as jnp
from jax import lax
from jax.experimental import pallas as pl
from jax.experimental.pallas import tpu as pltpu

F32 = jnp.float32
BF16 = jnp.bfloat16
I32 = jnp.int32

GRID_W = 64
POOL_WINDOWS = (2, 4, 8, 16)
CAPACITY_FACTOR = 2
N_MOD = 6
EPS = 1e-6
NEG = -1e30

LANE = 128
SUBLANE = 8
BF16_ROWS = 16
MIB = 1024 * 1024
Q_ROWS = 4
HALO = 16
GATHER_TOKENS = 256
GATHER_WIN = 64
EXPERT_GROUP = 8
LOG2E = 1.4426950408889634
COMBINE_TOKENS = 256
COMBINE_WIN = 64


def _cp(sem, vmem_mib=48):
    return pltpu.CompilerParams(dimension_semantics=sem, vmem_limit_bytes=vmem_mib * MIB)


def _dot(a, b):
    return jnp.dot(a, b, preferred_element_type=F32)


def _dot_nt(a, b):
    return lax.dot_general(a, b, (((1,), (1,)), ((), ())), preferred_element_type=F32)


def _dot_tn(a, b):
    return lax.dot_general(a, b, (((0,), (0,)), ((), ())), preferred_element_type=F32)


def _tile(n, pref):
    return pref if n % pref == 0 else n


def _onehot(hit):
    return jnp.where(hit, 1.0, 0.0).astype(BF16)


def _mod_kernel(c_ref, w_ref, b_ref, o_ref):
    c = c_ref[...]
    a = (c * jax.nn.sigmoid(c)).astype(BF16)
    o_ref[0] = _dot(a, w_ref[0].astype(BF16)) + b_ref[0]


def _adaln_all(cond, w_mod, b_mod):
    n_layers, d, cols = w_mod.shape
    r = cond.shape[0]
    tn = _tile(cols, 1024)
    return pl.pallas_call(
        _mod_kernel,
        grid=(n_layers, cols // tn),
        in_specs=[
            pl.BlockSpec((r, d), lambda l, j: (0, 0)),
            pl.BlockSpec((1, d, tn), lambda l, j: (l, 0, j)),
            pl.BlockSpec((1, 1, tn), lambda l, j: (l, 0, j)),
        ],
        out_specs=pl.BlockSpec((1, r, tn), lambda l, j: (l, 0, j)),
        out_shape=jax.ShapeDtypeStruct((n_layers, r, cols), F32),
        compiler_params=_cp(("arbitrary", "arbitrary")),
        name="adaln_mod",
    )(cond, w_mod, b_mod.reshape(n_layers, 1, cols))


def _mod_spec(mod_all, layer, row_of):
    return pl.BlockSpec((1, 1) + mod_all.shape[2:], lambda i, *_: (layer, row_of(i), 0, 0))


def _gain_spec(gain_all, layer):
    return pl.BlockSpec((1, 1, gain_all.shape[-1]), lambda *_: (layer, 0, 0))


def _rms(x, g):
    return x * lax.rsqrt(jnp.mean(x * x, axis=-1, keepdims=True) + EPS) * g


def _modulated_norm(x, gain, mod, shift_i, scale_i):
    return _rms(x, gain) * (1.0 + mod[scale_i:scale_i + 1]) + mod[shift_i:shift_i + 1]


def _norm_mod_kernel(x_ref, g_ref, mod_ref, o_ref):
    o_ref[0] = _modulated_norm(x_ref[0], g_ref[0], mod_ref[0, 0], 0, 1).astype(BF16)


def _norm_mod(x, gain_all, mod_all, layer, row_of):
    b, n, d = x.shape
    tm = _tile(n, 512)
    return pl.pallas_call(
        _norm_mod_kernel,
        grid=(b, n // tm),
        in_specs=[pl.BlockSpec((1, tm, d), lambda i, j: (i, j, 0)),
                  _gain_spec(gain_all, layer), _mod_spec(mod_all, layer, row_of)],
        out_specs=pl.BlockSpec((1, tm, d), lambda i, j: (i, j, 0)),
        out_shape=jax.ShapeDtypeStruct((b, n, d), BF16),
        compiler_params=_cp(("arbitrary", "arbitrary")),
        name="norm_mod",
    )(x, gain_all, mod_all)


def _proj_kernel(h_ref, w_ref, o_ref):
    o_ref[0] = _dot(h_ref[0], w_ref[0]).astype(o_ref.dtype)


def _proj(h, w_all, layer, col_off, n_cols, out_dtype):
    b, n, k = h.shape
    tm = _tile(n, 1024)
    tn = next(t for t in (1024, 512, 256, LANE) if n_cols % t == 0 and col_off % t == 0)
    c0 = col_off // tn
    return pl.pallas_call(
        _proj_kernel,
        grid=(b, n // tm, n_cols // tn),
        in_specs=[
            pl.BlockSpec((1, tm, k), lambda i, j, q: (i, j, 0)),
            pl.BlockSpec((1, k, tn), lambda i, j, q: (layer, 0, c0 + q)),
        ],
        out_specs=pl.BlockSpec((1, tm, tn), lambda i, j, q: (i, j, q)),
        out_shape=jax.ShapeDtypeStruct((b, n, n_cols), out_dtype),
        compiler_params=_cp(("arbitrary",) * 3),
        name="in_proj",
    )(h, w_all)


def _halo_specs(n, tm, k):
    assert tm % HALO == 0 and max(POOL_WINDOWS) // 2 <= HALO // 2
    per, last = tm // HALO, n // HALO - 1
    return [pl.BlockSpec((1, HALO, k), lambda i, j, q: (i, jnp.maximum(j * per - 1, 0), 0)),
            pl.BlockSpec((1, tm, k), lambda i, j, q: (i, j, 0)),
            pl.BlockSpec((1, HALO, k), lambda i, j, q: (i, jnp.minimum((j + 1) * per, last), 0))]


def _project_with_halo(hp_ref, h_ref, hn_ref, project):
    tm = h_ref.shape[1]
    u = project(jnp.concatenate([hp_ref[0], h_ref[0], hn_ref[0]], axis=0))
    j, nj = pl.program_id(1), pl.num_programs(1)
    row = lax.broadcasted_iota(I32, u.shape, 0)
    outside = ((j == 0) & (row < HALO)) | ((j == nj - 1) & (row >= tm + HALO))
    return jnp.where(outside, 0.0, u)


def _conv_branch_kernel(hp_ref, h_ref, hn_ref, wb_ref, wc_ref, wx_ref, cw_ref, o_ref):
    tm = h_ref.shape[1]
    u = _project_with_halo(hp_ref, h_ref, hn_ref, lambda a: _dot(a, wc_ref[0]) * _dot(a, wx_ref[0]))
    rows = u.shape[0]
    centre = slice(HALO, HALO + tm)
    w = cw_ref[0]
    conv = (pltpu.roll(u, 1, 0)[centre] * w[0:1] + u[centre] * w[1:2]
            + pltpu.roll(u, rows - 1, 0)[centre] * w[2:3])
    o_ref[0] = (_dot(h_ref[0], wb_ref[0]) * conv).astype(BF16)


def _conv_branch(h, w_all, conv_w_all, layer, dc):
    b, n, k = h.shape
    tm, tn = _tile(n, 1024), _tile(dc, 512)
    nb = dc // tn
    wblk = lambda off: pl.BlockSpec((1, k, tn), lambda i, j, q: (layer, 0, q + off))
    return pl.pallas_call(
        _conv_branch_kernel,
        grid=(b, n // tm, nb),
        in_specs=_halo_specs(n, tm, k) + [wblk(0), wblk(nb), wblk(2 * nb),
                                          pl.BlockSpec((1, conv_w_all.shape[1], tn), lambda i, j, q: (layer, 0, q))],
        out_specs=pl.BlockSpec((1, tm, tn), lambda i, j, q: (i, j, q)),
        out_shape=jax.ShapeDtypeStruct((b, n, dc), BF16),
        compiler_params=_cp(("arbitrary",) * 3, 56),
        name="conv_branch",
    )(h, h, h, w_all, w_all, w_all, conv_w_all)


def _pool_branch_kernel(hp_ref, h_ref, hn_ref, wi_ref, pw_ref, s_ref, o_ref, *, n):
    grp = pl.program_id(2)
    tm = h_ref.shape[1]
    u = _project_with_halo(hp_ref, h_ref, hn_ref, lambda a: _dot(a, wi_ref[0]))
    rows = u.shape[0]
    centre = slice(HALO, HALO + tm)
    t = lax.broadcasted_iota(I32, (tm, u.shape[1]), 0) + pl.program_id(1) * tm

    def shifted(x, s):
        return pltpu.roll(x, s % rows, 0)

    for gi, win in enumerate(POOL_WINDOWS):
        @pl.when(grp == gi)
        def _(win=win):
            w = shifted(u, 1) + u
            span = 2
            while span < win:
                half = span // 2
                w = shifted(w, half) + shifted(w, -half)
                span *= 2
            cnt = jnp.minimum(t + win // 2, n) - jnp.maximum(t - win // 2, 0)
            mixed = w[centre] / cnt.astype(F32) - u[centre]
            y = _dot(mixed.astype(BF16), pw_ref[0, 0]) * s_ref[0, 0]
            o_ref[0] = y.astype(BF16)


def _pool_branch(h, w_all, col_off, pool_w_all, pool_scale_all, layer):
    b, n, k = h.shape
    g, pg = pool_w_all.shape[1], pool_w_all.shape[2]
    assert g == len(POOL_WINDOWS) and col_off % pg == 0
    c0 = col_off // pg
    tm = _tile(n, 1024)
    return pl.pallas_call(
        functools.partial(_pool_branch_kernel, n=n),
        grid=(b, n // tm, g),
        in_specs=_halo_specs(n, tm, k) + [
            pl.BlockSpec((1, k, pg), lambda i, j, q: (layer, 0, c0 + q)),
            pl.BlockSpec((1, 1, pg, pg), lambda i, j, q: (layer, q, 0, 0)),
            pl.BlockSpec((1, 1, 1, pg), lambda i, j, q: (layer, q, 0, 0)),
        ],
        out_specs=pl.BlockSpec((1, tm, pg), lambda i, j, q: (i, j, q)),
        out_shape=jax.ShapeDtypeStruct((b, n, g * pg), BF16),
        compiler_params=_cp(("arbitrary",) * 3),
        name="pool_branch",
    )(h, h, h, w_all, pool_w_all, pool_scale_all)


def _group_geometry(rows, win_rows):
    key_rows = Q_ROWS + win_rows
    n_groups = rows // Q_ROWS
    assert rows % Q_ROWS == 0 and n_groups >= 3 and rows >= key_rows
    return key_rows, n_groups, (0, 1, n_groups - 1)


def _bias_kernel(rpb_ref, o_ref, *, rows, win_rows, win_cols):
    key_rows, _, reps = _group_geometry(rows, win_rows)
    w = GRID_W
    assert 2 * w == LANE
    qc = lax.broadcasted_iota(I32, (w, LANE), 0)
    lane = lax.broadcasted_iota(I32, (w, LANE), 1)
    first = lane < w
    kc = jnp.where(first, lane, lane - w)
    cstart = jnp.clip(qc - win_cols // 2, 0, w - win_cols)
    in_cols = (kc >= cstart) & (kc < cstart + win_cols)
    first8 = first[:SUBLANE]

    def bias_row(dr):
        if dr is None:
            return jnp.zeros((SUBLANE, LANE), F32)
        return jnp.broadcast_to(rpb_ref[0, 0, dr:dr + 1, :], (SUBLANE, LANE))

    def pair_block(dr_a, dr_b):
        if dr_a is None and dr_b is None:
            return jnp.full((w, LANE), NEG, F32)
        r = jnp.where(first8, bias_row(dr_a), pltpu.roll(bias_row(dr_b), w, 1))
        t = pltpu.roll(jnp.broadcast_to(r[0:1], (w, LANE)), LANE - (win_cols - 1), 1, stride=1, stride_axis=0)
        ok = in_cols if dr_a is not None and dr_b is not None else in_cols & (first if dr_b is None else ~first)
        return jnp.where(ok, t, NEG) * LOG2E

    for ti, g in enumerate(reps):
        kb = min(max(Q_ROWS * g - win_rows // 2, 0), rows - key_rows)
        for i in range(Q_ROWS):
            r = Q_ROWS * g + i
            rs = min(max(r - win_rows // 2, 0), rows - win_rows)
            drs = []
            for j in range(key_rows):
                kr = kb + j
                drs.append(kr - r + win_rows - 1 if rs <= kr < rs + win_rows else None)
            for jp in range(key_rows // 2):
                o_ref[0, 0, ti, i * w:(i + 1) * w, jp * 2 * w:(jp + 1) * 2 * w] = pair_block(
                    drs[2 * jp], drs[2 * jp + 1])


def _bias_tiles(rpb, rows):
    n_layers, n_heads, n_dr, n_dc = rpb.shape
    win_rows, win_cols = (n_dr + 1) // 2, (n_dc + 1) // 2
    key_rows, _, _ = _group_geometry(rows, win_rows)
    assert key_rows % 2 == 0
    tq, tk = Q_ROWS * GRID_W, key_rows * GRID_W
    rpb_lanes = jnp.pad(rpb, ((0, 0), (0, 0), (0, 0), (0, LANE - n_dc)))
    return pl.pallas_call(
        functools.partial(_bias_kernel, rows=rows, win_rows=win_rows, win_cols=win_cols),
        grid=(n_layers, n_heads),
        in_specs=[pl.BlockSpec((1, 1, n_dr, LANE), lambda l, h: (l, h, 0, 0))],
        out_specs=pl.BlockSpec((1, 1, 3, tq, tk), lambda l, h: (l, h, 0, 0, 0)),
        out_shape=jax.ShapeDtypeStruct((n_layers, n_heads, 3, tq, tk), F32),
        compiler_params=_cp(("arbitrary", "arbitrary")),
        name="bias_tiles",
    )(rpb_lanes)


def _natten_kernel(q_ref, k_ref, v_ref, kc_ref, vc_ref, bias_ref, o_ref, *, rows, win_rows):
    key_rows, n_groups, _ = _group_geometry(rows, win_rows)
    tq, tk = Q_ROWS * GRID_W, key_rows * GRID_W
    scale = q_ref.shape[-1] ** -0.5 * LOG2E
    kc = kc_ref[0]
    vc = vc_ref[0]

    def group(g, carry):
        kb = jnp.clip(Q_ROWS * g - win_rows // 2, 0, rows - key_rows)
        kind = jnp.where(g == 0, 0, jnp.where(g == n_groups - 1, 2, 1))
        q0 = pl.multiple_of(g * tq, tq)
        k0 = pl.multiple_of(kb * GRID_W, GRID_W)
        q = q_ref[0, pl.ds(q0, tq), :]
        kw = k_ref[0, pl.ds(k0, tk), :]
        vw = v_ref[0, pl.ds(k0, tk), :]
        s_win = _dot_nt(q, kw) * scale + bias_ref[0, 0, kind]
        s_ctx = _dot_nt(q, kc) * scale
        m = jnp.maximum(jnp.max(s_win, axis=-1, keepdims=True), jnp.max(s_ctx, axis=-1, keepdims=True))
        p_win = jnp.exp2(s_win - m)
        p_ctx = jnp.exp2(s_ctx - m)
        inv = 1.0 / (jnp.sum(p_win, axis=-1, keepdims=True) + jnp.sum(p_ctx, axis=-1, keepdims=True))
        o = (_dot(p_win.astype(BF16), vw) + _dot(p_ctx.astype(BF16), vc)) * inv
        o_ref[0, pl.ds(q0, tq), :] = o.astype(BF16)
        return carry

    lax.fori_loop(0, n_groups, group, 0, unroll=8 if n_groups % 8 == 0 else 2)


def _natten(qkv, kv_ctx, k_off, v_off, bias_all, layer, n_heads):
    b, n, c3 = qkv.shape
    dh = c3 // (3 * n_heads)
    lc = kv_ctx.shape[1]
    rows = n // GRID_W
    win_rows = bias_all.shape[-1] // GRID_W - Q_ROWS
    seq = lambda off: pl.BlockSpec((1, n, dh), lambda i, h: (i, 0, off + h))
    ctx = lambda off: pl.BlockSpec((1, lc, dh), lambda i, h: (i, 0, off + h))
    return pl.pallas_call(
        functools.partial(_natten_kernel, rows=rows, win_rows=win_rows),
        grid=(b, n_heads),
        in_specs=[
            seq(0), seq(n_heads), seq(2 * n_heads), ctx(k_off), ctx(v_off),
            pl.BlockSpec((1, 1) + bias_all.shape[2:], lambda i, h: (layer, h, 0, 0, 0)),
        ],
        out_specs=pl.BlockSpec((1, n, dh), lambda i, h: (i, 0, h)),
        out_shape=jax.ShapeDtypeStruct((b, n, n_heads * dh), BF16),
        compiler_params=_cp(("arbitrary", "arbitrary")),
        name="natten",
    )(qkv, qkv, qkv, kv_ctx, kv_ctx, bias_all)


def _ctx_attn_kernel(q_ref, k_ref, v_ref, o_ref):
    q = q_ref[0]
    s = _dot_nt(q, k_ref[0]) * (q.shape[-1] ** -0.5)
    p = jnp.exp(s - jnp.max(s, axis=-1, keepdims=True))
    p = p * (1.0 / jnp.sum(p, axis=-1, keepdims=True))
    o_ref[0] = _dot(p.astype(BF16), v_ref[0]).astype(BF16)


def _ctx_attention(qkv, n_heads):
    b, lc, c3 = qkv.shape
    dh = c3 // (3 * n_heads)
    blk = lambda off: pl.BlockSpec((1, lc, dh), lambda i, h: (i, 0, off + h))
    return pl.pallas_call(
        _ctx_attn_kernel,
        grid=(b, n_heads),
        in_specs=[blk(0), blk(n_heads), blk(2 * n_heads)],
        out_specs=blk(0),
        out_shape=jax.ShapeDtypeStruct((b, lc, n_heads * dh), BF16),
        compiler_params=_cp(("arbitrary", "arbitrary")),
        name="ctx_attention",
    )(qkv, qkv, qkv)


def _merge_kernel(h_ref, zc_ref, zp_ref, za_ref, g0_ref, g1_ref, g2_ref, wc_ref, wp_ref, wa_ref, o_ref):
    h = h_ref[0]
    m = jax.nn.sigmoid(_dot(h, g0_ref[0])) * _dot(zc_ref[0], wc_ref[0])
    m = m + jax.nn.sigmoid(_dot(h, g1_ref[0])) * _dot(zp_ref[0], wp_ref[0])
    m = m + jax.nn.sigmoid(_dot(h, g2_ref[0])) * _dot(za_ref[0], wa_ref[0])
    o_ref[0] = m.astype(BF16)


def _merge(h, zc, zp, za, w_in_all, gate_off, wc_all, wp_all, wa_all, layer):
    b, n, d = h.shape
    tm, tn = _tile(n, 512), _tile(d, 512)
    nb = d // tn
    assert gate_off % tn == 0
    g0 = gate_off // tn
    act = lambda a: pl.BlockSpec((1, tm, a.shape[-1]), lambda i, j, q: (i, j, 0))
    wcol = lambda a, off: pl.BlockSpec((1, a.shape[1], tn), lambda i, j, q: (layer, 0, q + off))
    return pl.pallas_call(
        _merge_kernel,
        grid=(b, n // tm, nb),
        in_specs=[act(h), act(zc), act(zp), act(za),
                  wcol(w_in_all, g0), wcol(w_in_all, g0 + nb), wcol(w_in_all, g0 + 2 * nb),
                  wcol(wc_all, 0), wcol(wp_all, 0), wcol(wa_all, 0)],
        out_specs=pl.BlockSpec((1, tm, tn), lambda i, j, q: (i, j, q)),
        out_shape=jax.ShapeDtypeStruct((b, n, d), BF16),
        compiler_params=_cp(("arbitrary",) * 3, 56),
        name="merge",
    )(h, zc, zp, za, w_in_all, w_in_all, w_in_all, wc_all, wp_all, wa_all)


def _out_proj_kernel(m_ref, w_ref, x_ref, mod_ref, g_ref, wr_ref, x1_ref, h2_ref, aff_ref):
    mod = mod_ref[0, 0]
    tm = m_ref.shape[1]
    n_parts = 2 if tm % (2 * LANE) == 0 else 1
    for p in range(n_parts):
        r = slice(p * tm // n_parts, (p + 1) * tm // n_parts)
        x1 = x_ref[0, r, :] + mod[2:3] * _dot(m_ref[0, r, :], w_ref[0])
        x1_ref[0, r, :] = x1
        h2 = _modulated_norm(x1, g_ref[0], mod, 3, 4).astype(BF16)
        h2_ref[0, r, :] = h2
    logits = _dot_nt(wr_ref[0], h2_ref[0])
    e = jnp.exp(logits - jnp.max(logits, axis=0, keepdims=True))
    aff_ref[0] = e / jnp.sum(e, axis=0, keepdims=True)


def _out_proj(merged, wo_all, x, mod_all, gain2_all, wr_t_all, layer, row_of):
    b, n, d = x.shape
    e = wr_t_all.shape[1]
    tm = _tile(n, 256)
    row = pl.BlockSpec((1, tm, d), lambda i, j: (i, j, 0))
    return pl.pallas_call(
        _out_proj_kernel,
        grid=(b, n // tm),
        in_specs=[row, pl.BlockSpec((1, d, d), lambda i, j: (layer, 0, 0)), row,
                  _mod_spec(mod_all, layer, row_of), _gain_spec(gain2_all, layer),
                  pl.BlockSpec((1, e, d), lambda i, j: (layer, 0, 0))],
        out_specs=[row, row, pl.BlockSpec((1, e, tm), lambda i, j: (i, 0, j))],
        out_shape=[jax.ShapeDtypeStruct((b, n, d), F32), jax.ShapeDtypeStruct((b, n, d), BF16),
                   jax.ShapeDtypeStruct((b, e, n), F32)],
        compiler_params=_cp(("arbitrary", "arbitrary")),
        name="out_proj",
    )(merged, wo_all, x, mod_all, gain2_all, wr_t_all)


def _lane_cumsum(x):
    e, n = x.shape
    nk = n // LANE
    i = lax.broadcasted_iota(I32, (LANE, LANE), 0)
    j = lax.broadcasted_iota(I32, (LANE, LANE), 1)
    upper = _onehot(i <= j)
    lane = lax.broadcasted_iota(I32, (e, LANE), 1)
    off = jnp.zeros((e, 1), F32)
    bounds = jnp.zeros((e, LANE), F32)
    parts = []
    for k in range(nk):
        bounds = jnp.where(lane == k, off, bounds)
        c = _dot(x[:, k * LANE:(k + 1) * LANE].astype(BF16), upper) + off
        parts.append(c)
        off = c[:, LANE - 1:LANE]
    bounds = jnp.where(lane == nk, off, bounds)
    return jnp.concatenate(parts, axis=1), bounds


def _route_kernel(aff_ref, pos_ref, cum_ref, *, cap):
    aff = aff_ref[0]
    n_exp = aff.shape[0]
    bits = pltpu.bitcast(aff, I32)

    def search(i, prefix):
        cand = prefix | jnp.left_shift(jnp.int32(1), 30 - i)
        cnt = jnp.sum(jnp.where(bits >= cand, 1.0, 0.0), axis=1, keepdims=True)
        return jnp.where(cnt >= cap, cand, prefix)

    thr = lax.fori_loop(0, 31, search, jnp.zeros((n_exp, 1), I32))
    gt = bits > thr
    eq = bits == thr
    need = cap - jnp.sum(jnp.where(gt, 1.0, 0.0), axis=1, keepdims=True)
    eq_f = jnp.where(eq, 1.0, 0.0)
    eq_rank, _ = _lane_cumsum(eq_f)
    sel = gt | (eq & (eq_rank - eq_f < need))
    sel_f = jnp.where(sel, 1.0, 0.0)
    rank, bounds = _lane_cumsum(sel_f)
    pos_ref[0] = jnp.where(sel, (rank - sel_f).astype(I32), -1)
    cum_ref[0] = bounds.astype(I32)


def _route(aff, cap):
    b, e, n = aff.shape
    assert n % LANE == 0 and n // LANE < LANE
    return pl.pallas_call(
        functools.partial(_route_kernel, cap=cap),
        grid=(b,),
        in_specs=[pl.BlockSpec((1, e, n), lambda i: (i, 0, 0))],
        out_specs=[pl.BlockSpec((1, e, n), lambda i: (i, 0, 0)),
                   pl.BlockSpec((1, e, LANE), lambda i: (i, 0, 0))],
        out_shape=[jax.ShapeDtypeStruct((b, e, n), I32), jax.ShapeDtypeStruct((b, e, LANE), I32)],
        compiler_params=_cp(("arbitrary",)),
        name="route",
    )(aff)


def _gather_kernel(cum_ref, h_ref, pos_ref, aff_ref, xs_ref, gs_ref, *, n_exp, win, chunks_per_step):
    b, grp, k = pl.program_id(0), pl.program_id(1), pl.program_id(2)
    n_grp, capp, _ = xs_ref.shape
    tokens = h_ref.shape[1]
    e0 = grp * n_grp
    slot_iota = lax.broadcasted_iota(I32, (win, tokens), 0)

    @pl.when(k == 0)
    def _():
        xs_ref[...] = jnp.zeros_like(xs_ref)
        gs_ref[...] = jnp.zeros_like(gs_ref)

    def win_start(e):
        lo = cum_ref[(b * n_exp + e) * LANE + k * chunks_per_step]
        return pl.multiple_of(jnp.minimum((lo // BF16_ROWS) * BF16_ROWS, capp - win), BF16_ROWS)

    rows0 = pl.multiple_of(e0, n_grp)
    pos = pos_ref[0, pl.ds(rows0, n_grp), :]
    aff = aff_ref[0, pl.ds(rows0, n_grp), :]
    hits = [pos[g:g + 1, :] == slot_iota + win_start(e0 + g) for g in range(n_grp)]
    picked = _dot(jnp.concatenate([_onehot(h) for h in hits], axis=0), h_ref[0]).astype(BF16)
    for g in range(n_grp):
        ws = win_start(e0 + g)
        xs_ref[g, pl.ds(ws, win), :] += picked[g * win:(g + 1) * win]
        gs_ref[g, pl.ds(ws, win), :] += jnp.sum(jnp.where(hits[g], aff[g:g + 1, :], 0.0), axis=1, keepdims=True)

    def long_run(g, carry):
        e = e0 + g
        ws = win_start(e)
        hi = cum_ref[(b * n_exp + e) * LANE + (k + 1) * chunks_per_step]

        def extra(j, c):
            first = ws + j * win
            wj = pl.multiple_of(jnp.minimum(first, capp - win), BF16_ROWS)
            sl = slot_iota + wj
            hit = (pos_ref[0, pl.ds(e, 1), :] == sl) & (sl >= first)
            xs_ref[g, pl.ds(wj, win), :] += _dot(_onehot(hit), h_ref[0]).astype(BF16)
            gs_ref[g, pl.ds(wj, win), :] += jnp.sum(
                jnp.where(hit, aff_ref[0, pl.ds(e, 1), :], 0.0), axis=1, keepdims=True)
            return c

        lax.fori_loop(1, (hi - ws + win - 1) // win, extra, 0)
        return carry

    longest = functools.reduce(jnp.maximum, [
        cum_ref[(b * n_exp + e0 + g) * LANE + (k + 1) * chunks_per_step] - win_start(e0 + g) for g in range(n_grp)])

    @pl.when(longest > win)
    def _():
        lax.fori_loop(0, n_grp, long_run, 0)


def _gather(h2, pos, aff, cum, capp):
    b, n, d = h2.shape
    e = pos.shape[1]
    tokens = min(GATHER_TOKENS, n)
    win = min(GATHER_WIN, capp)
    n_grp = min(EXPERT_GROUP, e)
    assert n % tokens == 0 and tokens % LANE == 0 and e % n_grp == 0
    assert capp % BF16_ROWS == 0 and win % BF16_ROWS == 0
    per_expert = pl.BlockSpec((1, e, tokens), lambda i, g, k, *_: (i, 0, k))
    grid_spec = pltpu.PrefetchScalarGridSpec(
        num_scalar_prefetch=1,
        grid=(b, e // n_grp, n // tokens),
        in_specs=[pl.BlockSpec((1, tokens, d), lambda i, g, k, *_: (i, k, 0)), per_expert, per_expert],
        out_specs=[pl.BlockSpec((n_grp, capp, d), lambda i, g, k, *_: (g, i, 0)),
                   pl.BlockSpec((n_grp, capp, 1), lambda i, g, k, *_: (g, i, 0))],
    )
    return pl.pallas_call(
        functools.partial(_gather_kernel, n_exp=e, win=win, chunks_per_step=tokens // LANE),
        grid_spec=grid_spec,
        out_shape=[jax.ShapeDtypeStruct((e, b * capp, d), BF16),
                   jax.ShapeDtypeStruct((e, b * capp, 1), F32)],
        compiler_params=_cp(("arbitrary",) * 3, 56),
        name="moe_gather",
    )(cum.reshape(-1), h2, pos, aff)


def _expert_up_kernel(*refs, n_in):
    x_refs, (wg_ref, wu_ref), o_refs = refs[:n_in], refs[n_in:n_in + 2], refs[n_in + 2:]
    wg = wg_ref[0, 0].astype(BF16)
    wu = wu_ref[0, 0].astype(BF16)
    for x_ref, o_ref in zip(x_refs, o_refs):
        x = x_ref[0]
        g = _dot(x, wg)
        o_ref[0] = (g * jax.nn.sigmoid(g) * _dot(x, wu)).astype(BF16)


def _expert_up(xs_list, w_gate_all, w_up_all, layer):
    e, _, d = xs_list[0].shape
    f = w_gate_all.shape[-1]
    tf = _tile(f, 512)
    wblk = pl.BlockSpec((1, 1, d, tf), lambda i, j: (layer, i, 0, j))
    return pl.pallas_call(
        functools.partial(_expert_up_kernel, n_in=len(xs_list)),
        grid=(e, f // tf),
        in_specs=[pl.BlockSpec((1, x.shape[1], d), lambda i, j: (i, 0, 0)) for x in xs_list] + [wblk, wblk],
        out_specs=[pl.BlockSpec((1, x.shape[1], tf), lambda i, j: (i, 0, j)) for x in xs_list],
        out_shape=[jax.ShapeDtypeStruct((e, x.shape[1], f), BF16) for x in xs_list],
        compiler_params=_cp(("arbitrary", "arbitrary"), 56),
        name="expert_up",
    )(*xs_list, w_gate_all, w_up_all)


def _expert_down_kernel(*refs, n_in):
    h_refs, g_refs, w_ref, o_refs = refs[:n_in], refs[n_in:2 * n_in], refs[2 * n_in], refs[2 * n_in + 1:]
    w = w_ref[0, 0].astype(BF16)
    for h_ref, g_ref, o_ref in zip(h_refs, g_refs, o_refs):
        o_ref[0] = (_dot(h_ref[0], w) * g_ref[0]).astype(BF16)


def _expert_down(hid_list, gs_list, w_down_all, layer):
    e, _, f = hid_list[0].shape
    d = w_down_all.shape[-1]
    tn = _tile(d, 512)
    return pl.pallas_call(
        functools.partial(_expert_down_kernel, n_in=len(hid_list)),
        grid=(e, d // tn),
        in_specs=[pl.BlockSpec((1, h.shape[1], f), lambda i, j: (i, 0, 0)) for h in hid_list]
        + [pl.BlockSpec((1, g.shape[1], 1), lambda i, j: (i, 0, 0)) for g in gs_list]
        + [pl.BlockSpec((1, 1, f, tn), lambda i, j: (layer, i, 0, j))],
        out_specs=[pl.BlockSpec((1, h.shape[1], tn), lambda i, j: (i, 0, j)) for h in hid_list],
        out_shape=[jax.ShapeDtypeStruct((e, h.shape[1], d), BF16) for h in hid_list],
        compiler_params=_cp(("arbitrary", "arbitrary"), 56),
        name="expert_down",
    )(*hid_list, *gs_list, w_down_all)


def _combine_kernel(cum_ref, pos_ref, x_ref, mod_ref, gain_ref, modn_ref, ys_hbm, *rest,
                    n_exp, capp, win, chunks_per_step, final):
    n_out = 1 if final else 2
    outs, (ybuf, xbuf, acc_ref, sem, xsem) = rest[:n_out], rest[n_out:]
    b, t = pl.program_id(0), pl.program_id(1)
    n_t = pl.num_programs(1)
    step = b * n_t + t
    n_steps = pl.num_programs(0) * n_t
    slot = lax.rem(step, 2)
    tokens = pos_ref.shape[2]
    slot_iota = lax.broadcasted_iota(I32, (win, tokens), 0)

    def win_start(bb, tt, e):
        lo = cum_ref[(bb * n_exp + e) * LANE + tt * chunks_per_step]
        return jnp.minimum((lo // BF16_ROWS) * BF16_ROWS, capp - win)

    def window_copy(bb, tt, e, s):
        src0 = pl.multiple_of(bb * capp + win_start(bb, tt, e), BF16_ROWS)
        return pltpu.make_async_copy(ys_hbm.at[e, pl.ds(src0, win), :],
                                     ybuf.at[s, pl.ds(e * win, win), :], sem.at[s])

    def start_windows(bb, tt, s):
        for e in range(n_exp):
            window_copy(bb, tt, e, s).start()

    @pl.when(step == 0)
    def _():
        start_windows(b, t, slot)

    @pl.when(step + 1 < n_steps)
    def _():
        nxt = step + 1
        start_windows(nxt // n_t, lax.rem(nxt, n_t), 1 - slot)

    onehot = jnp.concatenate(
        [_onehot(pos_ref[0, e:e + 1, :] == slot_iota + win_start(b, t, e)) for e in range(n_exp)], axis=0)
    for e in range(n_exp):
        window_copy(b, t, e, slot).wait()
    acc_ref[...] = _dot_tn(onehot, ybuf[slot])

    def long_run(e, carry):
        ws = win_start(b, t, e)
        hi = cum_ref[(b * n_exp + e) * LANE + (t + 1) * chunks_per_step]

        def extra(j, c):
            first = ws + j * win
            wj = jnp.minimum(first, capp - win)
            src0 = pl.multiple_of(b * capp + wj, BF16_ROWS)
            cp = pltpu.make_async_copy(ys_hbm.at[e, pl.ds(src0, win), :], xbuf, xsem)
            cp.start()
            cp.wait()
            sl = slot_iota + wj
            hit = (pos_ref[0, pl.ds(e, 1), :] == sl) & (sl >= first)
            acc_ref[...] += _dot_tn(_onehot(hit), xbuf[...])
            return c

        lax.fori_loop(1, (hi - ws + win - 1) // win, extra, 0)
        return carry

    longest = functools.reduce(jnp.maximum, [
        cum_ref[(b * n_exp + e) * LANE + (t + 1) * chunks_per_step] - win_start(b, t, e) for e in range(n_exp)])

    @pl.when(longest > win)
    def _():
        lax.fori_loop(0, n_exp, long_run, 0)

    y = x_ref[0] + mod_ref[0, 0][5:6] * acc_ref[...]
    if final:
        outs[0][0] = _rms(y, gain_ref[0])
    else:
        outs[0][0] = y
        outs[1][0] = _modulated_norm(y, gain_ref[0], modn_ref[0, 0], 0, 1).astype(BF16)


def _combine(ys, pos, cum, x1, mod_all, layer, row_of, capp, gain_all, gain_layer, final):
    b, n, d = x1.shape
    e = pos.shape[1]
    tokens = min(COMBINE_TOKENS, n)
    win = min(COMBINE_WIN, capp)
    assert n % tokens == 0 and tokens % LANE == 0 and capp % BF16_ROWS == 0 and win % BF16_ROWS == 0
    mod_next_layer = layer if final else layer + 1
    row = pl.BlockSpec((1, tokens, d), lambda i, t, *_: (i, t, 0))
    grid_spec = pltpu.PrefetchScalarGridSpec(
        num_scalar_prefetch=1,
        grid=(b, n // tokens),
        in_specs=[
            pl.BlockSpec((1, e, tokens), lambda i, t, *_: (i, 0, t)),
            row,
            _mod_spec(mod_all, layer, row_of),
            _gain_spec(gain_all, gain_layer),
            _mod_spec(mod_all, mod_next_layer, row_of),
            pl.BlockSpec(memory_space=pl.ANY),
        ],
        out_specs=[row] if final else [row, row],
        scratch_shapes=[pltpu.VMEM((2, e * win, d), BF16), pltpu.VMEM((win, d), BF16),
                        pltpu.VMEM((tokens, d), F32),
                        pltpu.SemaphoreType.DMA((2,)), pltpu.SemaphoreType.DMA],
    )
    out_f32 = jax.ShapeDtypeStruct((b, n, d), F32)
    return pl.pallas_call(
        functools.partial(_combine_kernel, n_exp=e, capp=capp, win=win,
                          chunks_per_step=tokens // LANE, final=final),
        grid_spec=grid_spec,
        out_shape=[out_f32] if final else [out_f32, jax.ShapeDtypeStruct((b, n, d), BF16)],
        compiler_params=_cp(("arbitrary", "arbitrary")),
        name="moe_combine",
    )(cum.reshape(-1), pos, x1, mod_all, gain_all, mod_all, ys)


def _route_and_gather(h2, aff):
    n, e = h2.shape[1], aff.shape[1]
    cap = max(1, CAPACITY_FACTOR * n // e)
    capp = -(-cap // BF16_ROWS) * BF16_ROWS
    pos, cum = _route(aff, cap)
    xs, gs = _gather(h2, pos, aff, cum, capp)
    return dict(xs=xs, gs=gs, pos=pos, cum=cum, capp=capp)


def kernel(x, c, ctx, c_ctx, w_mod, b_mod, norm1, norm2, w_in, conv_w, pool_w, pool_scale, rpb,
           w_conv_out, w_pool_out, w_attn_out, w_o, w_router, w_e_gate, w_e_up, w_e_down, final_norm):
    depth = w_mod.shape[0]
    b, n, d = x.shape
    dc, dp = conv_w.shape[-1], pool_scale.shape[-1]
    n_heads, da = rpb.shape[1], w_attn_out.shape[1]
    off_pool, off_q = 3 * dc, 3 * dc + dp
    off_k, off_gate = off_q + da, off_q + 3 * da
    assert n % GRID_W == 0 and da % n_heads == 0

    n_rows = -(-(b + 1) // SUBLANE) * SUBLANE
    cond = jnp.zeros((n_rows, d), F32).at[:b].set(c).at[b].set(c_ctx)
    mod_all = _adaln_all(cond, w_mod, b_mod).reshape(depth, n_rows, N_MOD, d)
    lat_row, ctx_row = (lambda i: i), (lambda i: b)
    bias_all = _bias_tiles(rpb, n // GRID_W)

    w_in_b = w_in.astype(BF16)
    wc_b, wp_b, wa_b, wo_b = (w.astype(BF16) for w in (w_conv_out, w_pool_out, w_attn_out, w_o))
    wr_t = jnp.swapaxes(w_router, 1, 2).astype(BF16)
    pool_w_b = pool_w.astype(BF16)
    pool_scale_r = pool_scale.reshape(depth, pool_w.shape[1], 1, -1)
    gain1, gain2 = norm1.reshape(depth, 1, d), norm2.reshape(depth, 1, d)
    gain_f = final_norm.reshape(1, 1, d)

    def mixer(h, attn, l, as_rows):
        zc = _conv_branch(h, w_in_b, conv_w, l, dc)
        zp = _pool_branch(h, w_in_b, off_pool, pool_w_b, pool_scale_r, l)
        return _merge(as_rows(h), as_rows(zc), as_rows(zp), as_rows(attn), w_in_b, off_gate, wc_b, wp_b, wa_b, l)

    joined = lambda a: a.reshape(1, -1, a.shape[-1])
    per_sample = lambda a: a.reshape(b, -1, a.shape[-1])
    keep = lambda a: a

    hx = _norm_mod(x, gain1, mod_all, 0, lat_row)
    hc = _norm_mod(ctx, gain1, mod_all, 0, ctx_row)
    for l in range(depth):
        last = l == depth - 1
        groups = []
        if last:
            kv_c = per_sample(_proj(joined(hc), w_in_b, l, off_k, 2 * da, BF16))
            k_off, v_off = 0, n_heads
        else:
            kv_c = per_sample(_proj(joined(hc), w_in_b, l, off_q, 3 * da, BF16))
            k_off, v_off = n_heads, 2 * n_heads
            merged_c = mixer(hc, _ctx_attention(kv_c, n_heads), l, joined)
            ctx1, hc2, aff_c = _out_proj(merged_c, wo_b, joined(ctx), mod_all, gain2, wr_t, l, ctx_row)
            ctx1, hc2 = per_sample(ctx1), per_sample(hc2)
            aff_c = jnp.swapaxes(aff_c.reshape(aff_c.shape[1], b, -1), 0, 1)
            groups.append(dict(_route_and_gather(hc2, aff_c), x1=ctx1, row_of=ctx_row))

        qkv = _proj(hx, w_in_b, l, off_q, 3 * da, BF16)
        attn = _natten(qkv, kv_c, k_off, v_off, bias_all, l, n_heads)
        x1, hx2, aff_x = _out_proj(mixer(hx, attn, l, keep), wo_b, x, mod_all, gain2, wr_t, l, lat_row)
        groups.append(dict(_route_and_gather(hx2, aff_x), x1=x1, row_of=lat_row))

        hid = _expert_up([g["xs"] for g in groups], w_e_gate, w_e_up, l)
        ys = _expert_down(hid, [g["gs"] for g in groups], w_e_down, l)
        res = [_combine(y, g["pos"], g["cum"], g["x1"], mod_all, l, g["row_of"], g["capp"],
                        gain_f if last else gain1, 0 if last else l + 1, last)
               for y, g in zip(ys, groups)]
        if last:
            return res[-1][0]
        (ctx, hc), (x, hx) = res
```

```python
import functools

import jax
import jax.numpy as jnp
from jax import lax
from jax.experimental import pallas as pl
from jax.experimental.pallas import tpu as pltpu

F32 = jnp.float32
BF16 = jnp.bfloat16
I32 = jnp.int32

GRID_W = 64
POOL_WINDOWS = (2, 4, 8, 16)
CAPACITY_FACTOR = 2
N_MOD = 6
EPS = 1e-6
NEG = -1e30

LANE = 128
SUBLANE = 8
BF16_ROWS = 16
MIB = 1024 * 1024
Q_ROWS = 4
HALO = 16
GATHER_TOKENS = 256
GATHER_WIN = 64
EXPERT_GROUP = 8
LOG2E = 1.4426950408889634
COMBINE_TOKENS = 256
COMBINE_WIN = 64


def _cp(sem, vmem_mib=48):
    return pltpu.CompilerParams(dimension_semantics=sem, vmem_limit_bytes=vmem_mib * MIB)


def _dot(a, b):
    return jnp.dot(a, b, preferred_element_type=F32)


def _dot_nt(a, b):
    return lax.dot_general(a, b, (((1,), (1,)), ((), ())), preferred_element_type=F32)


def _dot_tn(a, b):
    return lax.dot_general(a, b, (((0,), (0,)), ((), ())), preferred_element_type=F32)


def _tile(n, pref):
    return pref if n % pref == 0 else n


def _onehot(hit):
    return jnp.where(hit, 1.0, 0.0).astype(BF16)


def _mod_kernel(c_ref, w_ref, b_ref, o_ref):
    c = c_ref[...]
    a = (c * jax.nn.sigmoid(c)).astype(BF16)
    o_ref[0] = _dot(a, w_ref[0].astype(BF16)) + b_ref[0]


def _adaln_all(cond, w_mod, b_mod):
    n_layers, d, cols = w_mod.shape
    r = cond.shape[0]
    tn = _tile(cols, 1024)
    return pl.pallas_call(
        _mod_kernel,
        grid=(n_layers, cols // tn),
        in_specs=[
            pl.BlockSpec((r, d), lambda l, j: (0, 0)),
            pl.BlockSpec((1, d, tn), lambda l, j: (l, 0, j)),
            pl.BlockSpec((1, 1, tn), lambda l, j: (l, 0, j)),
        ],
        out_specs=pl.BlockSpec((1, r, tn), lambda l, j: (l, 0, j)),
        out_shape=jax.ShapeDtypeStruct((n_layers, r, cols), F32),
        compiler_params=_cp(("arbitrary", "arbitrary")),
        name="adaln_mod",
    )(cond, w_mod, b_mod.reshape(n_layers, 1, cols))


def _mod_spec(mod_all, layer, row_of):
    return pl.BlockSpec((1, 1) + mod_all.shape[2:], lambda i, *_: (layer, row_of(i), 0, 0))


def _gain_spec(gain_all, layer):
    return pl.BlockSpec((1, 1, gain_all.shape[-1]), lambda *_: (layer, 0, 0))


def _rms(x, g):
    return x * lax.rsqrt(jnp.mean(x * x, axis=-1, keepdims=True) + EPS) * g


def _modulated_norm(x, gain, mod, shift_i, scale_i):
    return _rms(x, gain) * (1.0 + mod[scale_i:scale_i + 1]) + mod[shift_i:shift_i + 1]


def _norm_mod_kernel(x_ref, g_ref, mod_ref, o_ref):
    o_ref[0] = _modulated_norm(x_ref[0], g_ref[0], mod_ref[0, 0], 0, 1).astype(BF16)


def _norm_mod(x, gain_all, mod_all, layer, row_of):
    b, n, d = x.shape
    tm = _tile(n, 512)
    return pl.pallas_call(
        _norm_mod_kernel,
        grid=(b, n // tm),
        in_specs=[pl.BlockSpec((1, tm, d), lambda i, j: (i, j, 0)),
                  _gain_spec(gain_all, layer), _mod_spec(mod_all, layer, row_of)],
        out_specs=pl.BlockSpec((1, tm, d), lambda i, j: (i, j, 0)),
        out_shape=jax.ShapeDtypeStruct((b, n, d), BF16),
        compiler_params=_cp(("arbitrary", "arbitrary")),
        name="norm_mod",
    )(x, gain_all, mod_all)


def _proj_kernel(h_ref, w_ref, o_ref):
    o_ref[0] = _dot(h_ref[0], w_ref[0]).astype(o_ref.dtype)


def _proj(h, w_all, layer, col_off, n_cols, out_dtype):
    b, n, k = h.shape
    tm = _tile(n, 1024)
    tn = next(t for t in (1024, 512, 256, LANE) if n_cols % t == 0 and col_off % t == 0)
    c0 = col_off // tn
    return pl.pallas_call(
        _proj_kernel,
        grid=(b, n // tm, n_cols // tn),
        in_specs=[
            pl.BlockSpec((1, tm, k), lambda i, j, q: (i, j, 0)),
            pl.BlockSpec((1, k, tn), lambda i, j, q: (layer, 0, c0 + q)),
        ],
        out_specs=pl.BlockSpec((1, tm, tn), lambda i, j, q: (i, j, q)),
        out_shape=jax.ShapeDtypeStruct((b, n, n_cols), out_dtype),
        compiler_params=_cp(("arbitrary",) * 3),
        name="in_proj",
    )(h, w_all)


def _halo_specs(n, tm, k):
    assert tm % HALO == 0 and max(POOL_WINDOWS) // 2 <= HALO // 2
    per, last = tm // HALO, n // HALO - 1
    return [pl.BlockSpec((1, HALO, k), lambda i, j, q: (i, jnp.maximum(j * per - 1, 0), 0)),
            pl.BlockSpec((1, tm, k), lambda i, j, q: (i, j, 0)),
            pl.BlockSpec((1, HALO, k), lambda i, j, q: (i, jnp.minimum((j + 1) * per, last), 0))]


def _project_with_halo(hp_ref, h_ref, hn_ref, project):
    tm = h_ref.shape[1]
    u = project(jnp.concatenate([hp_ref[0], h_ref[0], hn_ref[0]], axis=0))
    j, nj = pl.program_id(1), pl.num_programs(1)
    row = lax.broadcasted_iota(I32, u.shape, 0)
    outside = ((j == 0) & (row < HALO)) | ((j == nj - 1) & (row >= tm + HALO))
    return jnp.where(outside, 0.0, u)


def _conv_branch_kernel(hp_ref, h_ref, hn_ref, wb_ref, wc_ref, wx_ref, cw_ref, o_ref):
    tm = h_ref.shape[1]
    u = _project_with_halo(hp_ref, h_ref, hn_ref, lambda a: _dot(a, wc_ref[0]) * _dot(a, wx_ref[0]))
    rows = u.shape[0]
    centre = slice(HALO, HALO + tm)
    w = cw_ref[0]
    conv = (pltpu.roll(u, 1, 0)[centre] * w[0:1] + u[centre] * w[1:2]
            + pltpu.roll(u, rows - 1, 0)[centre] * w[2:3])
    o_ref[0] = (_dot(h_ref[0], wb_ref[0]) * conv).astype(BF16)


def _conv_branch(h, w_all, conv_w_all, layer, dc):
    b, n, k = h.shape
    tm, tn = _tile(n, 1024), _tile(dc, 512)
    nb = dc // tn
    wblk = lambda off: pl.BlockSpec((1, k, tn), lambda i, j, q: (layer, 0, q + off))
    return pl.pallas_call(
        _conv_branch_kernel,
        grid=(b, n // tm, nb),
        in_specs=_halo_specs(n, tm, k) + [wblk(0), wblk(nb), wblk(2 * nb),
                                          pl.BlockSpec((1, conv_w_all.shape[1], tn), lambda i, j, q: (layer, 0, q))],
        out_specs=pl.BlockSpec((1, tm, tn), lambda i, j, q: (i, j, q)),
        out_shape=jax.ShapeDtypeStruct((b, n, dc), BF16),
        compiler_params=_cp(("arbitrary",) * 3, 56),
        name="conv_branch",
    )(h, h, h, w_all, w_all, w_all, conv_w_all)


def _pool_branch_kernel(hp_ref, h_ref, hn_ref, wi_ref, pw_ref, s_ref, o_ref, *, n):
    grp = pl.program_id(2)
    tm = h_ref.shape[1]
    u = _project_with_halo(hp_ref, h_ref, hn_ref, lambda a: _dot(a, wi_ref[0]))
    rows = u.shape[0]
    centre = slice(HALO, HALO + tm)
    t = lax.broadcasted_iota(I32, (tm, u.shape[1]), 0) + pl.program_id(1) * tm

    def shifted(x, s):
        return pltpu.roll(x, s % rows, 0)

    for gi, win in enumerate(POOL_WINDOWS):
        @pl.when(grp == gi)
        def _(win=win):
            w = shifted(u, 1) + u
            span = 2
            while span < win:
                half = span // 2
                w = shifted(w, half) + shifted(w, -half)
                span *= 2
            cnt = jnp.minimum(t + win // 2, n) - jnp.maximum(t - win // 2, 0)
            mixed = w[centre] / cnt.astype(F32) - u[centre]
            y = _dot(mixed.astype(BF16), pw_ref[0, 0]) * s_ref[0, 0]
            o_ref[0] = y.astype(BF16)


def _pool_branch(h, w_all, col_off, pool_w_all, pool_scale_all, layer):
    b, n, k = h.shape
    g, pg = pool_w_all.shape[1], pool_w_all.shape[2]
    assert g == len(POOL_WINDOWS) and col_off % pg == 0
    c0 = col_off // pg
    tm = _tile(n, 1024)
    return pl.pallas_call(
        functools.partial(_pool_branch_kernel, n=n),
        grid=(b, n // tm, g),
        in_specs=_halo_specs(n, tm, k) + [
            pl.BlockSpec((1, k, pg), lambda i, j, q: (layer, 0, c0 + q)),
            pl.BlockSpec((1, 1, pg, pg), lambda i, j, q: (layer, q, 0, 0)),
            pl.BlockSpec((1, 1, 1, pg), lambda i, j, q: (layer, q, 0, 0)),
        ],
        out_specs=pl.BlockSpec((1, tm, pg), lambda i, j, q: (i, j, q)),
        out_shape=jax.ShapeDtypeStruct((b, n, g * pg), BF16),
        compiler_params=_cp(("arbitrary",) * 3),
        name="pool_branch",
    )(h, h, h, w_all, pool_w_all, pool_scale_all)


def _group_geometry(rows, win_rows):
    key_rows = Q_ROWS + win_rows
    n_groups = rows // Q_ROWS
    assert rows % Q_ROWS == 0 and n_groups >= 3 and rows >= key_rows
    return key_rows, n_groups, (0, 1, n_groups - 1)


def _bias_kernel(rpb_ref, o_ref, *, rows, win_rows, win_cols):
    key_rows, _, reps = _group_geometry(rows, win_rows)
    w = GRID_W
    assert 2 * w == LANE
    qc = lax.broadcasted_iota(I32, (w, LANE), 0)
    lane = lax.broadcasted_iota(I32, (w, LANE), 1)
    first = lane < w
    kc = jnp.where(first, lane, lane - w)
    cstart = jnp.clip(qc - win_cols // 2, 0, w - win_cols)
    in_cols = (kc >= cstart) & (kc < cstart + win_cols)
    first8 = first[:SUBLANE]

    def bias_row(dr):
        if dr is None:
            return jnp.zeros((SUBLANE, LANE), F32)
        return jnp.broadcast_to(rpb_ref[0, 0, dr:dr + 1, :], (SUBLANE, LANE))

    def pair_block(dr_a, dr_b):
        if dr_a is None and dr_b is None:
            return jnp.full((w, LANE), NEG, F32)
        r = jnp.where(first8, bias_row(dr_a), pltpu.roll(bias_row(dr_b), w, 1))
        t = pltpu.roll(jnp.broadcast_to(r[0:1], (w, LANE)), LANE - (win_cols - 1), 1, stride=1, stride_axis=0)
        ok = in_cols if dr_a is not None and dr_b is not None else in_cols & (first if dr_b is None else ~first)
        return jnp.where(ok, t, NEG) * LOG2E

    for ti, g in enumerate(reps):
        kb = min(max(Q_ROWS * g - win_rows // 2, 0), rows - key_rows)
        for i in range(Q_ROWS):
            r = Q_ROWS * g + i
            rs = min(max(r - win_rows // 2, 0), rows - win_rows)
            drs = []
            for j in range(key_rows):
                kr = kb + j
                drs.append(kr - r + win_rows - 1 if rs <= kr < rs + win_rows else None)
            for jp in range(key_rows // 2):
                o_ref[0, 0, ti, i * w:(i + 1) * w, jp * 2 * w:(jp + 1) * 2 * w] = pair_block(
                    drs[2 * jp], drs[2 * jp + 1])


def _bias_tiles(rpb, rows):
    n_layers, n_heads, n_dr, n_dc = rpb.shape
    win_rows, win_cols = (n_dr + 1) // 2, (n_dc + 1) // 2
    key_rows, _, _ = _group_geometry(rows, win_rows)
    assert key_rows % 2 == 0
    tq, tk = Q_ROWS * GRID_W, key_rows * GRID_W
    rpb_lanes = jnp.pad(rpb, ((0, 0), (0, 0), (0, 0), (0, LANE - n_dc)))
    return pl.pallas_call(
        functools.partial(_bias_kernel, rows=rows, win_rows=win_rows, win_cols=win_cols),
        grid=(n_layers, n_heads),
        in_specs=[pl.BlockSpec((1, 1, n_dr, LANE), lambda l, h: (l, h, 0, 0))],
        out_specs=pl.BlockSpec((1, 1, 3, tq, tk), lambda l, h: (l, h, 0, 0, 0)),
        out_shape=jax.ShapeDtypeStruct((n_layers, n_heads, 3, tq, tk), F32),
        compiler_params=_cp(("arbitrary", "arbitrary")),
        name="bias_tiles",
    )(rpb_lanes)


def _natten_kernel(q_ref, k_ref, v_ref, kc_ref, vc_ref, bias_ref, o_ref, vx_ref, vcx_ref, *, rows, win_rows):
    key_rows, n_groups, _ = _group_geometry(rows, win_rows)
    tq, tk = Q_ROWS * GRID_W, key_rows * GRID_W
    dh = q_ref.shape[-1]
    scale = dh ** -0.5 * LOG2E
    kc = kc_ref[0]
    for src, ext in ((v_ref, vx_ref), (vc_ref, vcx_ref)):
        ext[:, :dh] = src[0]
        ext[:, dh:] = jnp.ones((ext.shape[0], dh), BF16)

    def group(g, carry):
        kb = jnp.clip(Q_ROWS * g - win_rows // 2, 0, rows - key_rows)
        kind = jnp.where(g == 0, 0, jnp.where(g == n_groups - 1, 2, 1))
        q0 = pl.multiple_of(g * tq, tq)
        k0 = pl.multiple_of(kb * GRID_W, GRID_W)
        q = q_ref[0, pl.ds(q0, tq), :]
        kw = k_ref[0, pl.ds(k0, tk), :]
        s_win = _dot_nt(q, kw) * scale + bias_ref[0, 0, kind]
        s_ctx = _dot_nt(q, kc) * scale
        m = jnp.maximum(jnp.max(s_win, axis=-1, keepdims=True), jnp.max(s_ctx, axis=-1, keepdims=True))
        p_win = jnp.exp2(s_win - m).astype(BF16)
        p_ctx = jnp.exp2(s_ctx - m).astype(BF16)
        o = _dot(p_win, vx_ref[pl.ds(k0, tk), :]) + _dot(p_ctx, vcx_ref[...])
        o_ref[0, pl.ds(q0, tq), :] = (o[:, :dh] * (1.0 / o[:, dh:])).astype(BF16)
        return carry

    lax.fori_loop(0, n_groups, group, 0, unroll=8 if n_groups % 8 == 0 else 2)


def _natten(qkv, kv_ctx, k_off, v_off, bias_all, layer, n_heads):
    b, n, c3 = qkv.shape
    dh = c3 // (3 * n_heads)
    lc = kv_ctx.shape[1]
    rows = n // GRID_W
    win_rows = bias_all.shape[-1] // GRID_W - Q_ROWS
    seq = lambda off: pl.BlockSpec((1, n, dh), lambda i, h: (i, 0, off + h))
    ctx = lambda off: pl.BlockSpec((1, lc, dh), lambda i, h: (i, 0, off + h))
    return pl.pallas_call(
        functools.partial(_natten_kernel, rows=rows, win_rows=win_rows),
        grid=(b, n_heads),
        in_specs=[
            seq(0), seq(n_heads), seq(2 * n_heads), ctx(k_off), ctx(v_off),
            pl.BlockSpec((1, 1) + bias_all.shape[2:], lambda i, h: (layer, h, 0, 0, 0)),
        ],
        out_specs=pl.BlockSpec((1, n, dh), lambda i, h: (i, 0, h)),
        out_shape=jax.ShapeDtypeStruct((b, n, n_heads * dh), BF16),
        scratch_shapes=[pltpu.VMEM((n, 2 * dh), BF16), pltpu.VMEM((lc, 2 * dh), BF16)],
        compiler_params=_cp(("arbitrary", "arbitrary")),
        name="natten",
    )(qkv, qkv, qkv, kv_ctx, kv_ctx, bias_all)


def _ctx_attn_kernel(q_ref, k_ref, v_ref, o_ref):
    q = q_ref[0]
    s = _dot_nt(q, k_ref[0]) * (q.shape[-1] ** -0.5)
    p = jnp.exp(s - jnp.max(s, axis=-1, keepdims=True))
    p = p * (1.0 / jnp.sum(p, axis=-1, keepdims=True))
    o_ref[0] = _dot(p.astype(BF16), v_ref[0]).astype(BF16)


def _ctx_attention(qkv, n_heads):
    b, lc, c3 = qkv.shape
    dh = c3 // (3 * n_heads)
    blk = lambda off: pl.BlockSpec((1, lc, dh), lambda i, h: (i, 0, off + h))
    return pl.pallas_call(
        _ctx_attn_kernel,
        grid=(b, n_heads),
        in_specs=[blk(0), blk(n_heads), blk(2 * n_heads)],
        out_specs=blk(0),
        out_shape=jax.ShapeDtypeStruct((b, lc, n_heads * dh), BF16),
        compiler_params=_cp(("arbitrary", "arbitrary")),
        name="ctx_attention",
    )(qkv, qkv, qkv)


def _merge_kernel(h_ref, zc_ref, zp_ref, za_ref, g0_ref, g1_ref, g2_ref, wc_ref, wp_ref, wa_ref, o_ref):
    h = h_ref[0]
    m = jax.nn.sigmoid(_dot(h, g0_ref[0])) * _dot(zc_ref[0], wc_ref[0])
    m = m + jax.nn.sigmoid(_dot(h, g1_ref[0])) * _dot(zp_ref[0], wp_ref[0])
    m = m + jax.nn.sigmoid(_dot(h, g2_ref[0])) * _dot(za_ref[0], wa_ref[0])
    o_ref[0] = m.astype(BF16)


def _merge(h, zc, zp, za, w_in_all, gate_off, wc_all, wp_all, wa_all, layer):
    b, n, d = h.shape
    tm, tn = _tile(n, 512), _tile(d, 512)
    nb = d // tn
    assert gate_off % tn == 0
    g0 = gate_off // tn
    act = lambda a: pl.BlockSpec((1, tm, a.shape[-1]), lambda i, j, q: (i, j, 0))
    wcol = lambda a, off: pl.BlockSpec((1, a.shape[1], tn), lambda i, j, q: (layer, 0, q + off))
    return pl.pallas_call(
        _merge_kernel,
        grid=(b, n // tm, nb),
        in_specs=[act(h), act(zc), act(zp), act(za),
                  wcol(w_in_all, g0), wcol(w_in_all, g0 + nb), wcol(w_in_all, g0 + 2 * nb),
                  wcol(wc_all, 0), wcol(wp_all, 0), wcol(wa_all, 0)],
        out_specs=pl.BlockSpec((1, tm, tn), lambda i, j, q: (i, j, q)),
        out_shape=jax.ShapeDtypeStruct((b, n, d), BF16),
        compiler_params=_cp(("arbitrary",) * 3, 56),
        name="merge",
    )(h, zc, zp, za, w_in_all, w_in_all, w_in_all, wc_all, wp_all, wa_all)


def _out_proj_kernel(m_ref, w_ref, x_ref, mod_ref, g_ref, wr_ref, x1_ref, h2_ref, aff_ref):
    mod = mod_ref[0, 0]
    tm = m_ref.shape[1]
    n_parts = 2 if tm % (2 * LANE) == 0 else 1
    for p in range(n_parts):
        r = slice(p * tm // n_parts, (p + 1) * tm // n_parts)
        x1 = x_ref[0, r, :] + mod[2:3] * _dot(m_ref[0, r, :], w_ref[0])
        x1_ref[0, r, :] = x1
        h2 = _modulated_norm(x1, g_ref[0], mod, 3, 4).astype(BF16)
        h2_ref[0, r, :] = h2
    logits = _dot_nt(wr_ref[0], h2_ref[0])
    e = jnp.exp(logits - jnp.max(logits, axis=0, keepdims=True))
    aff_ref[0] = e / jnp.sum(e, axis=0, keepdims=True)


def _out_proj(merged, wo_all, x, mod_all, gain2_all, wr_t_all, layer, row_of):
    b, n, d = x.shape
    e = wr_t_all.shape[1]
    tm = _tile(n, 256)
    row = pl.BlockSpec((1, tm, d), lambda i, j: (i, j, 0))
    return pl.pallas_call(
        _out_proj_kernel,
        grid=(b, n // tm),
        in_specs=[row, pl.BlockSpec((1, d, d), lambda i, j: (layer, 0, 0)), row,
                  _mod_spec(mod_all, layer, row_of), _gain_spec(gain2_all, layer),
                  pl.BlockSpec((1, e, d), lambda i, j: (layer, 0, 0))],
        out_specs=[row, row, pl.BlockSpec((1, e, tm), lambda i, j: (i, 0, j))],
        out_shape=[jax.ShapeDtypeStruct((b, n, d), F32), jax.ShapeDtypeStruct((b, n, d), BF16),
                   jax.ShapeDtypeStruct((b, e, n), F32)],
        compiler_params=_cp(("arbitrary", "arbitrary")),
        name="out_proj",
    )(merged, wo_all, x, mod_all, gain2_all, wr_t_all)


def _lane_cumsum(x):
    e, n = x.shape
    nk = n // LANE
    i = lax.broadcasted_iota(I32, (LANE, LANE), 0)
    j = lax.broadcasted_iota(I32, (LANE, LANE), 1)
    upper = _onehot(i <= j)
    lane = lax.broadcasted_iota(I32, (e, LANE), 1)
    off = jnp.zeros((e, 1), F32)
    bounds = jnp.zeros((e, LANE), F32)
    parts = []
    for k in range(nk):
        bounds = jnp.where(lane == k, off, bounds)
        c = _dot(x[:, k * LANE:(k + 1) * LANE].astype(BF16), upper) + off
        parts.append(c)
        off = c[:, LANE - 1:LANE]
    bounds = jnp.where(lane == nk, off, bounds)
    return jnp.concatenate(parts, axis=1), bounds


def _route_kernel(aff_ref, pos_ref, cum_ref, *, cap):
    aff = aff_ref[0]
    n_exp = aff.shape[0]
    bits = pltpu.bitcast(aff, I32)

    def search(i, prefix):
        cand = prefix | jnp.left_shift(jnp.int32(1), 30 - i)
        cnt = jnp.sum(jnp.where(bits >= cand, 1.0, 0.0), axis=1, keepdims=True)
        return jnp.where(cnt >= cap, cand, prefix)

    thr = lax.fori_loop(0, 31, search, jnp.zeros((n_exp, 1), I32))
    gt = bits > thr
    eq = bits == thr
    need = cap - jnp.sum(jnp.where(gt, 1.0, 0.0), axis=1, keepdims=True)
    eq_f = jnp.where(eq, 1.0, 0.0)
    eq_rank, _ = _lane_cumsum(eq_f)
    sel = gt | (eq & (eq_rank - eq_f < need))
    sel_f = jnp.where(sel, 1.0, 0.0)
    rank, bounds = _lane_cumsum(sel_f)
    pos_ref[0] = jnp.where(sel, (rank - sel_f).astype(I32), -1)
    cum_ref[0] = bounds.astype(I32)


def _route(aff, cap):
    b, e, n = aff.shape
    assert n % LANE == 0 and n // LANE < LANE
    return pl.pallas_call(
        functools.partial(_route_kernel, cap=cap),
        grid=(b,),
        in_specs=[pl.BlockSpec((1, e, n), lambda i: (i, 0, 0))],
        out_specs=[pl.BlockSpec((1, e, n), lambda i: (i, 0, 0)),
                   pl.BlockSpec((1, e, LANE), lambda i: (i, 0, 0))],
        out_shape=[jax.ShapeDtypeStruct((b, e, n), I32), jax.ShapeDtypeStruct((b, e, LANE), I32)],
        compiler_params=_cp(("arbitrary",)),
        name="route",
    )(aff)


def _gather_kernel(cum_ref, h_ref, pos_ref, aff_ref, xs_ref, gs_ref, *, n_exp, win, chunks_per_step):
    b, grp, k = pl.program_id(0), pl.program_id(1), pl.program_id(2)
    n_grp, capp, _ = xs_ref.shape
    tokens = h_ref.shape[1]
    e0 = grp * n_grp
    slot_iota = lax.broadcasted_iota(I32, (win, tokens), 0)

    @pl.when(k == 0)
    def _():
        xs_ref[...] = jnp.zeros_like(xs_ref)
        gs_ref[...] = jnp.zeros_like(gs_ref)

    def win_start(e):
        lo = cum_ref[(b * n_exp + e) * LANE + k * chunks_per_step]
        return pl.multiple_of(jnp.minimum((lo // BF16_ROWS) * BF16_ROWS, capp - win), BF16_ROWS)

    rows0 = pl.multiple_of(e0, n_grp)
    pos = pos_ref[0, pl.ds(rows0, n_grp), :]
    aff = aff_ref[0, pl.ds(rows0, n_grp), :]
    hits = [pos[g:g + 1, :] == slot_iota + win_start(e0 + g) for g in range(n_grp)]
    picked = _dot(jnp.concatenate([_onehot(h) for h in hits], axis=0), h_ref[0]).astype(BF16)
    for g in range(n_grp):
        ws = win_start(e0 + g)
        xs_ref[g, pl.ds(ws, win), :] += picked[g * win:(g + 1) * win]
        gs_ref[g, pl.ds(ws, win), :] += jnp.sum(jnp.where(hits[g], aff[g:g + 1, :], 0.0), axis=1, keepdims=True)

    def long_run(g, carry):
        e = e0 + g
        ws = win_start(e)
        hi = cum_ref[(b * n_exp + e) * LANE + (k + 1) * chunks_per_step]

        def extra(j, c):
            first = ws + j * win
            wj = pl.multiple_of(jnp.minimum(first, capp - win), BF16_ROWS)
            sl = slot_iota + wj
            hit = (pos_ref[0, pl.ds(e, 1), :] == sl) & (sl >= first)
            xs_ref[g, pl.ds(wj, win), :] += _dot(_onehot(hit), h_ref[0]).astype(BF16)
            gs_ref[g, pl.ds(wj, win), :] += jnp.sum(
                jnp.where(hit, aff_ref[0, pl.ds(e, 1), :], 0.0), axis=1, keepdims=True)
            return c

        lax.fori_loop(1, (hi - ws + win - 1) // win, extra, 0)
        return carry

    longest = functools.reduce(jnp.maximum, [
        cum_ref[(b * n_exp + e0 + g) * LANE + (k + 1) * chunks_per_step] - win_start(e0 + g) for g in range(n_grp)])

    @pl.when(longest > win)
    def _():
        lax.fori_loop(0, n_grp, long_run, 0)


def _gather(h2, pos, aff, cum, capp):
    b, n, d = h2.shape
    e = pos.shape[1]
    tokens = min(GATHER_TOKENS, n)
    win = min(GATHER_WIN, capp)
    n_grp = min(EXPERT_GROUP, e)
    assert n % tokens == 0 and tokens % LANE == 0 and e % n_grp == 0
    assert capp % BF16_ROWS == 0 and win % BF16_ROWS == 0
    per_expert = pl.BlockSpec((1, e, tokens), lambda i, g, k, *_: (i, 0, k))
    grid_spec = pltpu.PrefetchScalarGridSpec(
        num_scalar_prefetch=1,
        grid=(b, e // n_grp, n // tokens),
        in_specs=[pl.BlockSpec((1, tokens, d), lambda i, g, k, *_: (i, k, 0)), per_expert, per_expert],
        out_specs=[pl.BlockSpec((n_grp, capp, d), lambda i, g, k, *_: (g, i, 0)),
                   pl.BlockSpec((n_grp, capp, 1), lambda i, g, k, *_: (g, i, 0))],
    )
    return pl.pallas_call(
        functools.partial(_gather_kernel, n_exp=e, win=win, chunks_per_step=tokens // LANE),
        grid_spec=grid_spec,
        out_shape=[jax.ShapeDtypeStruct((e, b * capp, d), BF16),
                   jax.ShapeDtypeStruct((e, b * capp, 1), F32)],
        compiler_params=_cp(("arbitrary",) * 3, 56),
        name="moe_gather",
    )(cum.reshape(-1), h2, pos, aff)


def _expert_up_kernel(*refs, n_in):
    x_refs, (wg_ref, wu_ref), o_refs = refs[:n_in], refs[n_in:n_in + 2], refs[n_in + 2:]
    wg = wg_ref[0, 0].astype(BF16)
    wu = wu_ref[0, 0].astype(BF16)
    for x_ref, o_ref in zip(x_refs, o_refs):
        x = x_ref[0]
        g = _dot(x, wg)
        o_ref[0] = (g * jax.nn.sigmoid(g) * _dot(x, wu)).astype(BF16)


def _expert_up(xs_list, w_gate_all, w_up_all, layer):
    e, _, d = xs_list[0].shape
    f = w_gate_all.shape[-1]
    tf = _tile(f, 512)
    wblk = pl.BlockSpec((1, 1, d, tf), lambda i, j: (layer, i, 0, j))
    return pl.pallas_call(
        functools.partial(_expert_up_kernel, n_in=len(xs_list)),
        grid=(e, f // tf),
        in_specs=[pl.BlockSpec((1, x.shape[1], d), lambda i, j: (i, 0, 0)) for x in xs_list] + [wblk, wblk],
        out_specs=[pl.BlockSpec((1, x.shape[1], tf), lambda i, j: (i, 0, j)) for x in xs_list],
        out_shape=[jax.ShapeDtypeStruct((e, x.shape[1], f), BF16) for x in xs_list],
        compiler_params=_cp(("arbitrary", "arbitrary"), 56),
        name="expert_up",
    )(*xs_list, w_gate_all, w_up_all)


def _expert_down_kernel(*refs, n_in):
    h_refs, g_refs, w_ref, o_refs = refs[:n_in], refs[n_in:2 * n_in], refs[2 * n_in], refs[2 * n_in + 1:]
    w = w_ref[0, 0].astype(BF16)
    for h_ref, g_ref, o_ref in zip(h_refs, g_refs, o_refs):
        o_ref[0] = (_dot(h_ref[0], w) * g_ref[0]).astype(BF16)


def _expert_down(hid_list, gs_list, w_down_all, layer):
    e, _, f = hid_list[0].shape
    d = w_down_all.shape[-1]
    tn = _tile(d, 512)
    return pl.pallas_call(
        functools.partial(_expert_down_kernel, n_in=len(hid_list)),
        grid=(e, d // tn),
        in_specs=[pl.BlockSpec((1, h.shape[1], f), lambda i, j: (i, 0, 0)) for h in hid_list]
        + [pl.BlockSpec((1, g.shape[1], 1), lambda i, j: (i, 0, 0)) for g in gs_list]
        + [pl.BlockSpec((1, 1, f, tn), lambda i, j: (layer, i, 0, j))],
        out_specs=[pl.BlockSpec((1, h.shape[1], tn), lambda i, j: (i, 0, j)) for h in hid_list],
        out_shape=[jax.ShapeDtypeStruct((e, h.shape[1], d), BF16) for h in hid_list],
        compiler_params=_cp(("arbitrary", "arbitrary"), 56),
        name="expert_down",
    )(*hid_list, *gs_list, w_down_all)


def _combine_kernel(cum_ref, pos_ref, x_ref, mod_ref, gain_ref, modn_ref, ys_hbm, *rest,
                    n_exp, capp, win, chunks_per_step, final):
    n_out = 1 if final else 2
    outs, (ybuf, xbuf, acc_ref, sem, xsem) = rest[:n_out], rest[n_out:]
    b, t = pl.program_id(0), pl.program_id(1)
    n_t = pl.num_programs(1)
    step = b * n_t + t
    n_steps = pl.num_programs(0) * n_t
    slot = lax.rem(step, 2)
    tokens = pos_ref.shape[2]
    slot_iota = lax.broadcasted_iota(I32, (win, tokens), 0)

    def win_start(bb, tt, e):
        lo = cum_ref[(bb * n_exp + e) * LANE + tt * chunks_per_step]
        return jnp.minimum((lo // BF16_ROWS) * BF16_ROWS, capp - win)

    def window_copy(bb, tt, e, s):
        src0 = pl.multiple_of(bb * capp + win_start(bb, tt, e), BF16_ROWS)
        return pltpu.make_async_copy(ys_hbm.at[e, pl.ds(src0, win), :],
                                     ybuf.at[s, pl.ds(e * win, win), :], sem.at[s])

    def start_windows(bb, tt, s):
        for e in range(n_exp):
            window_copy(bb, tt, e, s).start()

    @pl.when(step == 0)
    def _():
        start_windows(b, t, slot)

    @pl.when(step + 1 < n_steps)
    def _():
        nxt = step + 1
        start_windows(nxt // n_t, lax.rem(nxt, n_t), 1 - slot)

    onehot = jnp.concatenate(
        [_onehot(pos_ref[0, e:e + 1, :] == slot_iota + win_start(b, t, e)) for e in range(n_exp)], axis=0)
    for e in range(n_exp):
        window_copy(b, t, e, slot).wait()
    acc_ref[...] = _dot_tn(onehot, ybuf[slot])

    def long_run(e, carry):
        ws = win_start(b, t, e)
        hi = cum_ref[(b * n_exp + e) * LANE + (t + 1) * chunks_per_step]

        def extra(j, c):
            first = ws + j * win
            wj = jnp.minimum(first, capp - win)
            src0 = pl.multiple_of(b * capp + wj, BF16_ROWS)
            cp = pltpu.make_async_copy(ys_hbm.at[e, pl.ds(src0, win), :], xbuf, xsem)
            cp.start()
            cp.wait()
            sl = slot_iota + wj
            hit = (pos_ref[0, pl.ds(e, 1), :] == sl) & (sl >= first)
            acc_ref[...] += _dot_tn(_onehot(hit), xbuf[...])
            return c

        lax.fori_loop(1, (hi - ws + win - 1) // win, extra, 0)
        return carry

    longest = functools.reduce(jnp.maximum, [
        cum_ref[(b * n_exp + e) * LANE + (t + 1) * chunks_per_step] - win_start(b, t, e) for e in range(n_exp)])

    @pl.when(longest > win)
    def _():
        lax.fori_loop(0, n_exp, long_run, 0)

    y = x_ref[0] + mod_ref[0, 0][5:6] * acc_ref[...]
    if final:
        outs[0][0] = _rms(y, gain_ref[0])
    else:
        outs[0][0] = y
        outs[1][0] = _modulated_norm(y, gain_ref[0], modn_ref[0, 0], 0, 1).astype(BF16)


def _combine(ys, pos, cum, x1, mod_all, layer, row_of, capp, gain_all, gain_layer, final):
    b, n, d = x1.shape
    e = pos.shape[1]
    tokens = min(COMBINE_TOKENS, n)
    win = min(COMBINE_WIN, capp)
    assert n % tokens == 0 and tokens % LANE == 0 and capp % BF16_ROWS == 0 and win % BF16_ROWS == 0
    mod_next_layer = layer if final else layer + 1
    row = pl.BlockSpec((1, tokens, d), lambda i, t, *_: (i, t, 0))
    grid_spec = pltpu.PrefetchScalarGridSpec(
        num_scalar_prefetch=1,
        grid=(b, n // tokens),
        in_specs=[
            pl.BlockSpec((1, e, tokens), lambda i, t, *_: (i, 0, t)),
            row,
            _mod_spec(mod_all, layer, row_of),
            _gain_spec(gain_all, gain_layer),
            _mod_spec(mod_all, mod_next_layer, row_of),
            pl.BlockSpec(memory_space=pl.ANY),
        ],
        out_specs=[row] if final else [row, row],
        scratch_shapes=[pltpu.VMEM((2, e * win, d), BF16), pltpu.VMEM((win, d), BF16),
                        pltpu.VMEM((tokens, d), F32),
                        pltpu.SemaphoreType.DMA((2,)), pltpu.SemaphoreType.DMA],
    )
    out_f32 = jax.ShapeDtypeStruct((b, n, d), F32)
    return pl.pallas_call(
        functools.partial(_combine_kernel, n_exp=e, capp=capp, win=win,
                          chunks_per_step=tokens // LANE, final=final),
        grid_spec=grid_spec,
        out_shape=[out_f32] if final else [out_f32, jax.ShapeDtypeStruct((b, n, d), BF16)],
        compiler_params=_cp(("arbitrary", "arbitrary")),
        name="moe_combine",
    )(cum.reshape(-1), pos, x1, mod_all, gain_all, mod_all, ys)


def _route_and_gather(h2, aff):
    n, e = h2.shape[1], aff.shape[1]
    cap = max(1, CAPACITY_FACTOR * n // e)
    capp = -(-cap // BF16_ROWS) * BF16_ROWS
    pos, cum = _route(aff, cap)
    xs, gs = _gather(h2, pos, aff, cum, capp)
    return dict(xs=xs, gs=gs, pos=pos, cum=cum, capp=capp)


def kernel(x, c, ctx, c_ctx, w_mod, b_mod, norm1, norm2, w_in, conv_w, pool_w, pool_scale, rpb,
           w_conv_out, w_pool_out, w_attn_out, w_o, w_router, w_e_gate, w_e_up, w_e_down, final_norm):
    depth = w_mod.shape[0]
    b, n, d = x.shape
    dc, dp = conv_w.shape[-1], pool_scale.shape[-1]
    n_heads, da = rpb.shape[1], w_attn_out.shape[1]
    off_pool, off_q = 3 * dc, 3 * dc + dp
    off_k, off_gate = off_q + da, off_q + 3 * da
    assert n % GRID_W == 0 and da % n_heads == 0

    n_rows = -(-(b + 1) // SUBLANE) * SUBLANE
    cond = jnp.zeros((n_rows, d), F32).at[:b].set(c).at[b].set(c_ctx)
    mod_all = _adaln_all(cond, w_mod, b_mod).reshape(depth, n_rows, N_MOD, d)
    lat_row, ctx_row = (lambda i: i), (lambda i: b)
    bias_all = _bias_tiles(rpb, n // GRID_W)

    w_in_b = w_in.astype(BF16)
    wc_b, wp_b, wa_b, wo_b = (w.astype(BF16) for w in (w_conv_out, w_pool_out, w_attn_out, w_o))
    wr_t = jnp.swapaxes(w_router, 1, 2).astype(BF16)
    pool_w_b = pool_w.astype(BF16)
    pool_scale_r = pool_scale.reshape(depth, pool_w.shape[1], 1, -1)
    gain1, gain2 = norm1.reshape(depth, 1, d), norm2.reshape(depth, 1, d)
    gain_f = final_norm.reshape(1, 1, d)

    def mixer(h, attn, l, as_rows):
        zc = _conv_branch(h, w_in_b, conv_w, l, dc)
        zp = _pool_branch(h, w_in_b, off_pool, pool_w_b, pool_scale_r, l)
        return _merge(as_rows(h), as_rows(zc), as_rows(zp), as_rows(attn), w_in_b, off_gate, wc_b, wp_b, wa_b, l)

    joined = lambda a: a.reshape(1, -1, a.shape[-1])
    per_sample = lambda a: a.reshape(b, -1, a.shape[-1])
    keep = lambda a: a

    hx = _norm_mod(x, gain1, mod_all, 0, lat_row)
    hc = _norm_mod(ctx, gain1, mod_all, 0, ctx_row)
    for l in range(depth):
        last = l == depth - 1
        groups = []
        if last:
            kv_c = per_sample(_proj(joined(hc), w_in_b, l, off_k, 2 * da, BF16))
            k_off, v_off = 0, n_heads
        else:
            kv_c = per_sample(_proj(joined(hc), w_in_b, l, off_q, 3 * da, BF16))
            k_off, v_off = n_heads, 2 * n_heads
            merged_c = mixer(hc, _ctx_attention(kv_c, n_heads), l, joined)
            ctx1, hc2, aff_c = _out_proj(merged_c, wo_b, joined(ctx), mod_all, gain2, wr_t, l, ctx_row)
            ctx1, hc2 = per_sample(ctx1), per_sample(hc2)
            aff_c = jnp.swapaxes(aff_c.reshape(aff_c.shape[1], b, -1), 0, 1)
            groups.append(dict(_route_and_gather(hc2, aff_c), x1=ctx1, row_of=ctx_row))

        qkv = _proj(hx, w_in_b, l, off_q, 3 * da, BF16)
        attn = _natten(qkv, kv_c, k_off, v_off, bias_all, l, n_heads)
        x1, hx2, aff_x = _out_proj(mixer(hx, attn, l, keep), wo_b, x, mod_all, gain2, wr_t, l, lat_row)
        groups.append(dict(_route_and_gather(hx2, aff_x), x1=x1, row_of=lat_row))

        hid = _expert_up([g["xs"] for g in groups], w_e_gate, w_e_up, l)
        ys = _expert_down(hid, [g["gs"] for g in groups], w_e_down, l)
        res = [_combine(y, g["pos"], g["cum"], g["x1"], mod_all, l, g["row_of"], g["capp"],
                        gain_f if last else gain1, 0 if last else l + 1, last)
               for y, g in zip(ys, groups)]
        if last:
            return res[-1][0]
        (ctx, hc), (x, hx) = res
```

```python
import functools

import jax
import jax.numpy as jnp
from jax import lax
from jax.experimental import pallas as pl
from jax.experimental.pallas import tpu as pltpu

F32 = jnp.float32
BF16 = jnp.bfloat16
I32 = jnp.int32

GRID_W = 64
POOL_WINDOWS = (2, 4, 8, 16)
CAPACITY_FACTOR = 2
N_MOD = 6
EPS = 1e-6
NEG = -1e30

LANE = 128
SUBLANE = 8
BF16_ROWS = 16
MIB = 1024 * 1024
Q_ROWS = 4
HALO = 16
GATHER_TOKENS = 256
GATHER_WIN = 64
EXPERT_GROUP = 8
LOG2E = 1.4426950408889634
COMBINE_TOKENS = 256
COMBINE_WIN = 64


def _cp(sem, vmem_mib=48):
    return pltpu.CompilerParams(dimension_semantics=sem, vmem_limit_bytes=vmem_mib * MIB)


def _dot(a, b):
    return jnp.dot(a, b, preferred_element_type=F32)


def _dot_nt(a, b):
    return lax.dot_general(a, b, (((1,), (1,)), ((), ())), preferred_element_type=F32)


def _dot_tn(a, b):
    return lax.dot_general(a, b, (((0,), (0,)), ((), ())), preferred_element_type=F32)


def _tile(n, pref):
    return pref if n % pref == 0 else n


def _onehot(hit):
    return jnp.where(hit, 1.0, 0.0).astype(BF16)


def _mod_kernel(c_ref, w_ref, b_ref, o_ref):
    c = c_ref[...]
    a = (c * jax.nn.sigmoid(c)).astype(BF16)
    o_ref[0] = _dot(a, w_ref[0].astype(BF16)) + b_ref[0]


def _adaln_all(cond, w_mod, b_mod):
    n_layers, d, cols = w_mod.shape
    r = cond.shape[0]
    tn = _tile(cols, 1024)
    return pl.pallas_call(
        _mod_kernel,
        grid=(n_layers, cols // tn),
        in_specs=[
            pl.BlockSpec((r, d), lambda l, j: (0, 0)),
            pl.BlockSpec((1, d, tn), lambda l, j: (l, 0, j)),
            pl.BlockSpec((1, 1, tn), lambda l, j: (l, 0, j)),
        ],
        out_specs=pl.BlockSpec((1, r, tn), lambda l, j: (l, 0, j)),
        out_shape=jax.ShapeDtypeStruct((n_layers, r, cols), F32),
        compiler_params=_cp(("arbitrary", "arbitrary")),
        name="adaln_mod",
    )(cond, w_mod, b_mod.reshape(n_layers, 1, cols))


def _mod_spec(mod_all, layer, row_of):
    return pl.BlockSpec((1, 1) + mod_all.shape[2:], lambda i, *_: (layer, row_of(i), 0, 0))


def _gain_spec(gain_all, layer):
    return pl.BlockSpec((1, 1, gain_all.shape[-1]), lambda *_: (layer, 0, 0))


def _rms(x, g):
    return x * lax.rsqrt(jnp.mean(x * x, axis=-1, keepdims=True) + EPS) * g


def _modulated_norm(x, gain, mod, shift_i, scale_i):
    return _rms(x, gain) * (1.0 + mod[scale_i:scale_i + 1]) + mod[shift_i:shift_i + 1]


def _norm_mod_kernel(x_ref, g_ref, mod_ref, o_ref):
    o_ref[0] = _modulated_norm(x_ref[0], g_ref[0], mod_ref[0, 0], 0, 1).astype(BF16)


def _norm_mod(x, gain_all, mod_all, layer, row_of):
    b, n, d = x.shape
    tm = _tile(n, 512)
    return pl.pallas_call(
        _norm_mod_kernel,
        grid=(b, n // tm),
        in_specs=[pl.BlockSpec((1, tm, d), lambda i, j: (i, j, 0)),
                  _gain_spec(gain_all, layer), _mod_spec(mod_all, layer, row_of)],
        out_specs=pl.BlockSpec((1, tm, d), lambda i, j: (i, j, 0)),
        out_shape=jax.ShapeDtypeStruct((b, n, d), BF16),
        compiler_params=_cp(("arbitrary", "arbitrary")),
        name="norm_mod",
    )(x, gain_all, mod_all)


def _proj_kernel(h_ref, w_ref, o_ref):
    o_ref[0] = _dot(h_ref[0], w_ref[0]).astype(o_ref.dtype)


def _proj(h, w_all, layer, col_off, n_cols, out_dtype):
    b, n, k = h.shape
    tm = _tile(n, 1024)
    tn = next(t for t in (1024, 512, 256, LANE) if n_cols % t == 0 and col_off % t == 0)
    c0 = col_off // tn
    return pl.pallas_call(
        _proj_kernel,
        grid=(b, n // tm, n_cols // tn),
        in_specs=[
            pl.BlockSpec((1, tm, k), lambda i, j, q: (i, j, 0)),
            pl.BlockSpec((1, k, tn), lambda i, j, q: (layer, 0, c0 + q)),
        ],
        out_specs=pl.BlockSpec((1, tm, tn), lambda i, j, q: (i, j, q)),
        out_shape=jax.ShapeDtypeStruct((b, n, n_cols), out_dtype),
        compiler_params=_cp(("arbitrary",) * 3),
        name="in_proj",
    )(h, w_all)


def _halo_specs(n, tm, k):
    assert tm % HALO == 0 and max(POOL_WINDOWS) // 2 <= HALO // 2
    per, last = tm // HALO, n // HALO - 1
    return [pl.BlockSpec((1, HALO, k), lambda i, j, q: (i, jnp.maximum(j * per - 1, 0), 0)),
            pl.BlockSpec((1, tm, k), lambda i, j, q: (i, j, 0)),
            pl.BlockSpec((1, HALO, k), lambda i, j, q: (i, jnp.minimum((j + 1) * per, last), 0))]


def _project_with_halo(hp_ref, h_ref, hn_ref, project):
    tm = h_ref.shape[1]
    u = project(jnp.concatenate([hp_ref[0], h_ref[0], hn_ref[0]], axis=0))
    j, nj = pl.program_id(1), pl.num_programs(1)
    row = lax.broadcasted_iota(I32, u.shape, 0)
    outside = ((j == 0) & (row < HALO)) | ((j == nj - 1) & (row >= tm + HALO))
    return jnp.where(outside, 0.0, u)


def _conv_branch_kernel(hp_ref, h_ref, hn_ref, wb_ref, wc_ref, wx_ref, cw_ref, o_ref):
    tm = h_ref.shape[1]
    u = _project_with_halo(hp_ref, h_ref, hn_ref, lambda a: _dot(a, wc_ref[0]) * _dot(a, wx_ref[0]))
    rows = u.shape[0]
    centre = slice(HALO, HALO + tm)
    w = cw_ref[0]
    conv = (pltpu.roll(u, 1, 0)[centre] * w[0:1] + u[centre] * w[1:2]
            + pltpu.roll(u, rows - 1, 0)[centre] * w[2:3])
    o_ref[0] = (_dot(h_ref[0], wb_ref[0]) * conv).astype(BF16)


def _conv_branch(h, w_all, conv_w_all, layer, dc):
    b, n, k = h.shape
    tm, tn = _tile(n, 1024), _tile(dc, 512)
    nb = dc // tn
    wblk = lambda off: pl.BlockSpec((1, k, tn), lambda i, j, q: (layer, 0, q + off))
    return pl.pallas_call(
        _conv_branch_kernel,
        grid=(b, n // tm, nb),
        in_specs=_halo_specs(n, tm, k) + [wblk(0), wblk(nb), wblk(2 * nb),
                                          pl.BlockSpec((1, conv_w_all.shape[1], tn), lambda i, j, q: (layer, 0, q))],
        out_specs=pl.BlockSpec((1, tm, tn), lambda i, j, q: (i, j, q)),
        out_shape=jax.ShapeDtypeStruct((b, n, dc), BF16),
        compiler_params=_cp(("arbitrary",) * 3, 56),
        name="conv_branch",
    )(h, h, h, w_all, w_all, w_all, conv_w_all)


def _pool_branch_kernel(hp_ref, h_ref, hn_ref, wi_ref, pw_ref, s_ref, o_ref, *, n):
    grp = pl.program_id(2)
    tm = h_ref.shape[1]
    u = _project_with_halo(hp_ref, h_ref, hn_ref, lambda a: _dot(a, wi_ref[0]))
    rows = u.shape[0]
    centre = slice(HALO, HALO + tm)
    t = lax.broadcasted_iota(I32, (tm, u.shape[1]), 0) + pl.program_id(1) * tm

    def shifted(x, s):
        return pltpu.roll(x, s % rows, 0)

    for gi, win in enumerate(POOL_WINDOWS):
        @pl.when(grp == gi)
        def _(win=win):
            w = shifted(u, 1) + u
            span = 2
            while span < win:
                half = span // 2
                w = shifted(w, half) + shifted(w, -half)
                span *= 2
            cnt = jnp.minimum(t + win // 2, n) - jnp.maximum(t - win // 2, 0)
            mixed = w[centre] / cnt.astype(F32) - u[centre]
            y = _dot(mixed.astype(BF16), pw_ref[0, 0]) * s_ref[0, 0]
            o_ref[0] = y.astype(BF16)


def _pool_branch(h, w_all, col_off, pool_w_all, pool_scale_all, layer):
    b, n, k = h.shape
    g, pg = pool_w_all.shape[1], pool_w_all.shape[2]
    assert g == len(POOL_WINDOWS) and col_off % pg == 0
    c0 = col_off // pg
    tm = _tile(n, 1024)
    return pl.pallas_call(
        functools.partial(_pool_branch_kernel, n=n),
        grid=(b, n // tm, g),
        in_specs=_halo_specs(n, tm, k) + [
            pl.BlockSpec((1, k, pg), lambda i, j, q: (layer, 0, c0 + q)),
            pl.BlockSpec((1, 1, pg, pg), lambda i, j, q: (layer, q, 0, 0)),
            pl.BlockSpec((1, 1, 1, pg), lambda i, j, q: (layer, q, 0, 0)),
        ],
        out_specs=pl.BlockSpec((1, tm, pg), lambda i, j, q: (i, j, q)),
        out_shape=jax.ShapeDtypeStruct((b, n, g * pg), BF16),
        compiler_params=_cp(("arbitrary",) * 3),
        name="pool_branch",
    )(h, h, h, w_all, pool_w_all, pool_scale_all)


def _group_geometry(rows, win_rows):
    key_rows = Q_ROWS + win_rows
    n_groups = rows // Q_ROWS
    assert rows % Q_ROWS == 0 and n_groups >= 3 and rows >= key_rows
    return key_rows, n_groups, (0, 1, n_groups - 1)


def _bias_kernel(rpb_ref, o_ref, *, rows, win_rows, win_cols):
    key_rows, _, reps = _group_geometry(rows, win_rows)
    w = GRID_W
    assert 2 * w == LANE
    qc = lax.broadcasted_iota(I32, (w, LANE), 0)
    lane = lax.broadcasted_iota(I32, (w, LANE), 1)
    first = lane < w
    kc = jnp.where(first, lane, lane - w)
    cstart = jnp.clip(qc - win_cols // 2, 0, w - win_cols)
    in_cols = (kc >= cstart) & (kc < cstart + win_cols)
    first8 = first[:SUBLANE]

    def bias_row(dr):
        if dr is None:
            return jnp.zeros((SUBLANE, LANE), F32)
        return jnp.broadcast_to(rpb_ref[0, 0, dr:dr + 1, :], (SUBLANE, LANE))

    def pair_block(dr_a, dr_b):
        if dr_a is None and dr_b is None:
            return jnp.full((w, LANE), NEG, F32)
        r = jnp.where(first8, bias_row(dr_a), pltpu.roll(bias_row(dr_b), w, 1))
        t = pltpu.roll(jnp.broadcast_to(r[0:1], (w, LANE)), LANE - (win_cols - 1), 1, stride=1, stride_axis=0)
        ok = in_cols if dr_a is not None and dr_b is not None else in_cols & (first if dr_b is None else ~first)
        return jnp.where(ok, t, NEG) * LOG2E

    for ti, g in enumerate(reps):
        kb = min(max(Q_ROWS * g - win_rows // 2, 0), rows - key_rows)
        for i in range(Q_ROWS):
            r = Q_ROWS * g + i
            rs = min(max(r - win_rows // 2, 0), rows - win_rows)
            drs = []
            for j in range(key_rows):
                kr = kb + j
                drs.append(kr - r + win_rows - 1 if rs <= kr < rs + win_rows else None)
            for jp in range(key_rows // 2):
                o_ref[0, 0, ti, i * w:(i + 1) * w, jp * 2 * w:(jp + 1) * 2 * w] = pair_block(
                    drs[2 * jp], drs[2 * jp + 1])


def _bias_tiles(rpb, rows):
    n_layers, n_heads, n_dr, n_dc = rpb.shape
    win_rows, win_cols = (n_dr + 1) // 2, (n_dc + 1) // 2
    key_rows, _, _ = _group_geometry(rows, win_rows)
    assert key_rows % 2 == 0
    tq, tk = Q_ROWS * GRID_W, key_rows * GRID_W
    rpb_lanes = jnp.pad(rpb, ((0, 0), (0, 0), (0, 0), (0, LANE - n_dc)))
    return pl.pallas_call(
        functools.partial(_bias_kernel, rows=rows, win_rows=win_rows, win_cols=win_cols),
        grid=(n_layers, n_heads),
        in_specs=[pl.BlockSpec((1, 1, n_dr, LANE), lambda l, h: (l, h, 0, 0))],
        out_specs=pl.BlockSpec((1, 1, 3, tq, tk), lambda l, h: (l, h, 0, 0, 0)),
        out_shape=jax.ShapeDtypeStruct((n_layers, n_heads, 3, tq, tk), F32),
        compiler_params=_cp(("arbitrary", "arbitrary")),
        name="bias_tiles",
    )(rpb_lanes)


def _natten_kernel(q_ref, k_ref, v_ref, kc_ref, vc_ref, bias_ref, o_ref, vx_ref, vcx_ref, *, rows, win_rows):
    key_rows, n_groups, _ = _group_geometry(rows, win_rows)
    tq, tk = Q_ROWS * GRID_W, key_rows * GRID_W
    dh = q_ref.shape[-1]
    scale = dh ** -0.5 * LOG2E
    kc = kc_ref[0]
    for src, ext in ((v_ref, vx_ref), (vc_ref, vcx_ref)):
        ext[:, :dh] = src[0]
        ext[:, dh:] = jnp.ones((ext.shape[0], dh), BF16)

    def group(g, carry):
        kb = jnp.clip(Q_ROWS * g - win_rows // 2, 0, rows - key_rows)
        kind = jnp.where(g == 0, 0, jnp.where(g == n_groups - 1, 2, 1))
        q0 = pl.multiple_of(g * tq, tq)
        k0 = pl.multiple_of(kb * GRID_W, GRID_W)
        q = q_ref[0, pl.ds(q0, tq), :]
        kw = k_ref[0, pl.ds(k0, tk), :]
        s_win = _dot_nt(q, kw) * scale + bias_ref[0, 0, kind]
        s_ctx = _dot_nt(q, kc) * scale
        m = jnp.maximum(jnp.max(s_win, axis=-1, keepdims=True), jnp.max(s_ctx, axis=-1, keepdims=True))
        p_win = jnp.exp2(s_win - m).astype(BF16)
        p_ctx = jnp.exp2(s_ctx - m).astype(BF16)
        o = _dot(p_win, vx_ref[pl.ds(k0, tk), :]) + _dot(p_ctx, vcx_ref[...])
        o_ref[0, pl.ds(q0, tq), :] = (o[:, :dh] * (1.0 / o[:, dh:])).astype(BF16)
        return carry

    lax.fori_loop(0, n_groups, group, 0, unroll=16 if n_groups % 16 == 0 else 2)


def _natten(qkv, kv_ctx, k_off, v_off, bias_all, layer, n_heads):
    b, n, c3 = qkv.shape
    dh = c3 // (3 * n_heads)
    lc = kv_ctx.shape[1]
    rows = n // GRID_W
    win_rows = bias_all.shape[-1] // GRID_W - Q_ROWS
    seq = lambda off: pl.BlockSpec((1, n, dh), lambda i, h: (i, 0, off + h))
    ctx = lambda off: pl.BlockSpec((1, lc, dh), lambda i, h: (i, 0, off + h))
    return pl.pallas_call(
        functools.partial(_natten_kernel, rows=rows, win_rows=win_rows),
        grid=(b, n_heads),
        in_specs=[
            seq(0), seq(n_heads), seq(2 * n_heads), ctx(k_off), ctx(v_off),
            pl.BlockSpec((1, 1) + bias_all.shape[2:], lambda i, h: (layer, h, 0, 0, 0)),
        ],
        out_specs=pl.BlockSpec((1, n, dh), lambda i, h: (i, 0, h)),
        out_shape=jax.ShapeDtypeStruct((b, n, n_heads * dh), BF16),
        scratch_shapes=[pltpu.VMEM((n, 2 * dh), BF16), pltpu.VMEM((lc, 2 * dh), BF16)],
        compiler_params=_cp(("arbitrary", "arbitrary")),
        name="natten",
    )(qkv, qkv, qkv, kv_ctx, kv_ctx, bias_all)


def _ctx_attn_kernel(q_ref, k_ref, v_ref, o_ref):
    q = q_ref[0]
    s = _dot_nt(q, k_ref[0]) * (q.shape[-1] ** -0.5)
    p = jnp.exp(s - jnp.max(s, axis=-1, keepdims=True))
    p = p * (1.0 / jnp.sum(p, axis=-1, keepdims=True))
    o_ref[0] = _dot(p.astype(BF16), v_ref[0]).astype(BF16)


def _ctx_attention(qkv, n_heads):
    b, lc, c3 = qkv.shape
    dh = c3 // (3 * n_heads)
    blk = lambda off: pl.BlockSpec((1, lc, dh), lambda i, h: (i, 0, off + h))
    return pl.pallas_call(
        _ctx_attn_kernel,
        grid=(b, n_heads),
        in_specs=[blk(0), blk(n_heads), blk(2 * n_heads)],
        out_specs=blk(0),
        out_shape=jax.ShapeDtypeStruct((b, lc, n_heads * dh), BF16),
        compiler_params=_cp(("arbitrary", "arbitrary")),
        name="ctx_attention",
    )(qkv, qkv, qkv)


def _merge_kernel(h_ref, zc_ref, zp_ref, za_ref, g0_ref, g1_ref, g2_ref, wc_ref, wp_ref, wa_ref, o_ref):
    h = h_ref[0]
    m = jax.nn.sigmoid(_dot(h, g0_ref[0])) * _dot(zc_ref[0], wc_ref[0])
    m = m + jax.nn.sigmoid(_dot(h, g1_ref[0])) * _dot(zp_ref[0], wp_ref[0])
    m = m + jax.nn.sigmoid(_dot(h, g2_ref[0])) * _dot(za_ref[0], wa_ref[0])
    o_ref[0] = m.astype(BF16)


def _merge(h, zc, zp, za, w_in_all, gate_off, wc_all, wp_all, wa_all, layer):
    b, n, d = h.shape
    tm, tn = _tile(n, 512), _tile(d, 512)
    nb = d // tn
    assert gate_off % tn == 0
    g0 = gate_off // tn
    act = lambda a: pl.BlockSpec((1, tm, a.shape[-1]), lambda i, j, q: (i, j, 0))
    wcol = lambda a, off: pl.BlockSpec((1, a.shape[1], tn), lambda i, j, q: (layer, 0, q + off))
    return pl.pallas_call(
        _merge_kernel,
        grid=(b, n // tm, nb),
        in_specs=[act(h), act(zc), act(zp), act(za),
                  wcol(w_in_all, g0), wcol(w_in_all, g0 + nb), wcol(w_in_all, g0 + 2 * nb),
                  wcol(wc_all, 0), wcol(wp_all, 0), wcol(wa_all, 0)],
        out_specs=pl.BlockSpec((1, tm, tn), lambda i, j, q: (i, j, q)),
        out_shape=jax.ShapeDtypeStruct((b, n, d), BF16),
        compiler_params=_cp(("arbitrary",) * 3, 56),
        name="merge",
    )(h, zc, zp, za, w_in_all, w_in_all, w_in_all, wc_all, wp_all, wa_all)


def _out_proj_kernel(m_ref, w_ref, x_ref, mod_ref, g_ref, wr_ref, x1_ref, h2_ref, aff_ref):
    mod = mod_ref[0, 0]
    tm = m_ref.shape[1]
    n_parts = 2 if tm % (2 * LANE) == 0 else 1
    for p in range(n_parts):
        r = slice(p * tm // n_parts, (p + 1) * tm // n_parts)
        x1 = x_ref[0, r, :] + mod[2:3] * _dot(m_ref[0, r, :], w_ref[0])
        x1_ref[0, r, :] = x1
        h2 = _modulated_norm(x1, g_ref[0], mod, 3, 4).astype(BF16)
        h2_ref[0, r, :] = h2
    logits = _dot_nt(wr_ref[0], h2_ref[0])
    e = jnp.exp(logits - jnp.max(logits, axis=0, keepdims=True))
    aff_ref[0] = e / jnp.sum(e, axis=0, keepdims=True)


def _out_proj(merged, wo_all, x, mod_all, gain2_all, wr_t_all, layer, row_of):
    b, n, d = x.shape
    e = wr_t_all.shape[1]
    tm = _tile(n, 256)
    row = pl.BlockSpec((1, tm, d), lambda i, j: (i, j, 0))
    return pl.pallas_call(
        _out_proj_kernel,
        grid=(b, n // tm),
        in_specs=[row, pl.BlockSpec((1, d, d), lambda i, j: (layer, 0, 0)), row,
                  _mod_spec(mod_all, layer, row_of), _gain_spec(gain2_all, layer),
                  pl.BlockSpec((1, e, d), lambda i, j: (layer, 0, 0))],
        out_specs=[row, row, pl.BlockSpec((1, e, tm), lambda i, j: (i, 0, j))],
        out_shape=[jax.ShapeDtypeStruct((b, n, d), F32), jax.ShapeDtypeStruct((b, n, d), BF16),
                   jax.ShapeDtypeStruct((b, e, n), F32)],
        compiler_params=_cp(("arbitrary", "arbitrary")),
        name="out_proj",
    )(merged, wo_all, x, mod_all, gain2_all, wr_t_all)


def _lane_cumsum(x):
    e, n = x.shape
    nk = n // LANE
    i = lax.broadcasted_iota(I32, (LANE, LANE), 0)
    j = lax.broadcasted_iota(I32, (LANE, LANE), 1)
    upper = _onehot(i <= j)
    lane = lax.broadcasted_iota(I32, (e, LANE), 1)
    off = jnp.zeros((e, 1), F32)
    bounds = jnp.zeros((e, LANE), F32)
    parts = []
    for k in range(nk):
        bounds = jnp.where(lane == k, off, bounds)
        c = _dot(x[:, k * LANE:(k + 1) * LANE].astype(BF16), upper) + off
        parts.append(c)
        off = c[:, LANE - 1:LANE]
    bounds = jnp.where(lane == nk, off, bounds)
    return jnp.concatenate(parts, axis=1), bounds


def _route_kernel(aff_ref, pos_ref, cum_ref, *, cap):
    aff = aff_ref[0]
    n_exp = aff.shape[0]
    bits = pltpu.bitcast(aff, I32)

    def search(i, prefix):
        cand = prefix | jnp.left_shift(jnp.int32(1), 30 - i)
        cnt = jnp.sum(jnp.where(bits >= cand, 1.0, 0.0), axis=1, keepdims=True)
        return jnp.where(cnt >= cap, cand, prefix)

    thr = lax.fori_loop(0, 31, search, jnp.zeros((n_exp, 1), I32))
    gt = bits > thr
    eq = bits == thr
    need = cap - jnp.sum(jnp.where(gt, 1.0, 0.0), axis=1, keepdims=True)
    eq_f = jnp.where(eq, 1.0, 0.0)
    eq_rank, _ = _lane_cumsum(eq_f)
    sel = gt | (eq & (eq_rank - eq_f < need))
    sel_f = jnp.where(sel, 1.0, 0.0)
    rank, bounds = _lane_cumsum(sel_f)
    pos_ref[0] = jnp.where(sel, (rank - sel_f).astype(I32), -1)
    cum_ref[0] = bounds.astype(I32)


def _route(aff, cap):
    b, e, n = aff.shape
    assert n % LANE == 0 and n // LANE < LANE
    return pl.pallas_call(
        functools.partial(_route_kernel, cap=cap),
        grid=(b,),
        in_specs=[pl.BlockSpec((1, e, n), lambda i: (i, 0, 0))],
        out_specs=[pl.BlockSpec((1, e, n), lambda i: (i, 0, 0)),
                   pl.BlockSpec((1, e, LANE), lambda i: (i, 0, 0))],
        out_shape=[jax.ShapeDtypeStruct((b, e, n), I32), jax.ShapeDtypeStruct((b, e, LANE), I32)],
        compiler_params=_cp(("arbitrary",)),
        name="route",
    )(aff)


def _gather_kernel(cum_ref, h_ref, pos_ref, aff_ref, xs_ref, gs_ref, *, n_exp, win, chunks_per_step):
    b, grp, k = pl.program_id(0), pl.program_id(1), pl.program_id(2)
    n_grp, capp, _ = xs_ref.shape
    tokens = h_ref.shape[1]
    e0 = grp * n_grp
    slot_iota = lax.broadcasted_iota(I32, (win, tokens), 0)

    @pl.when(k == 0)
    def _():
        xs_ref[...] = jnp.zeros_like(xs_ref)
        gs_ref[...] = jnp.zeros_like(gs_ref)

    def win_start(e):
        lo = cum_ref[(b * n_exp + e) * LANE + k * chunks_per_step]
        return pl.multiple_of(jnp.minimum((lo // BF16_ROWS) * BF16_ROWS, capp - win), BF16_ROWS)

    rows0 = pl.multiple_of(e0, n_grp)
    pos = pos_ref[0, pl.ds(rows0, n_grp), :]
    aff = aff_ref[0, pl.ds(rows0, n_grp), :]
    hits = [pos[g:g + 1, :] == slot_iota + win_start(e0 + g) for g in range(n_grp)]
    picked = _dot(jnp.concatenate([_onehot(h) for h in hits], axis=0), h_ref[0]).astype(BF16)
    for g in range(n_grp):
        ws = win_start(e0 + g)
        xs_ref[g, pl.ds(ws, win), :] += picked[g * win:(g + 1) * win]
        gs_ref[g, pl.ds(ws, win), :] += jnp.sum(jnp.where(hits[g], aff[g:g + 1, :], 0.0), axis=1, keepdims=True)

    def long_run(g, carry):
        e = e0 + g
        ws = win_start(e)
        hi = cum_ref[(b * n_exp + e) * LANE + (k + 1) * chunks_per_step]

        def extra(j, c):
            first = ws + j * win
            wj = pl.multiple_of(jnp.minimum(first, capp - win), BF16_ROWS)
            sl = slot_iota + wj
            hit = (pos_ref[0, pl.ds(e, 1), :] == sl) & (sl >= first)
            xs_ref[g, pl.ds(wj, win), :] += _dot(_onehot(hit), h_ref[0]).astype(BF16)
            gs_ref[g, pl.ds(wj, win), :] += jnp.sum(
                jnp.where(hit, aff_ref[0, pl.ds(e, 1), :], 0.0), axis=1, keepdims=True)
            return c

        lax.fori_loop(1, (hi - ws + win - 1) // win, extra, 0)
        return carry

    longest = functools.reduce(jnp.maximum, [
        cum_ref[(b * n_exp + e0 + g) * LANE + (k + 1) * chunks_per_step] - win_start(e0 + g) for g in range(n_grp)])

    @pl.when(longest > win)
    def _():
        lax.fori_loop(0, n_grp, long_run, 0)


def _gather(h2, pos, aff, cum, capp):
    b, n, d = h2.shape
    e = pos.shape[1]
    tokens = min(GATHER_TOKENS, n)
    win = min(GATHER_WIN, capp)
    n_grp = min(EXPERT_GROUP, e)
    assert n % tokens == 0 and tokens % LANE == 0 and e % n_grp == 0
    assert capp % BF16_ROWS == 0 and win % BF16_ROWS == 0
    per_expert = pl.BlockSpec((1, e, tokens), lambda i, g, k, *_: (i, 0, k))
    grid_spec = pltpu.PrefetchScalarGridSpec(
        num_scalar_prefetch=1,
        grid=(b, e // n_grp, n // tokens),
        in_specs=[pl.BlockSpec((1, tokens, d), lambda i, g, k, *_: (i, k, 0)), per_expert, per_expert],
        out_specs=[pl.BlockSpec((n_grp, capp, d), lambda i, g, k, *_: (g, i, 0)),
                   pl.BlockSpec((n_grp, capp, 1), lambda i, g, k, *_: (g, i, 0))],
    )
    return pl.pallas_call(
        functools.partial(_gather_kernel, n_exp=e, win=win, chunks_per_step=tokens // LANE),
        grid_spec=grid_spec,
        out_shape=[jax.ShapeDtypeStruct((e, b * capp, d), BF16),
                   jax.ShapeDtypeStruct((e, b * capp, 1), F32)],
        compiler_params=_cp(("arbitrary",) * 3, 56),
        name="moe_gather",
    )(cum.reshape(-1), h2, pos, aff)


def _expert_up_kernel(*refs, n_in):
    x_refs, (wg_ref, wu_ref), o_refs = refs[:n_in], refs[n_in:n_in + 2], refs[n_in + 2:]
    wg = wg_ref[0, 0].astype(BF16)
    wu = wu_ref[0, 0].astype(BF16)
    for x_ref, o_ref in zip(x_refs, o_refs):
        x = x_ref[0]
        g = _dot(x, wg)
        o_ref[0] = (g * jax.nn.sigmoid(g) * _dot(x, wu)).astype(BF16)


def _expert_up(xs_list, w_gate_all, w_up_all, layer):
    e, _, d = xs_list[0].shape
    f = w_gate_all.shape[-1]
    tf = _tile(f, 512)
    wblk = pl.BlockSpec((1, 1, d, tf), lambda i, j: (layer, i, 0, j))
    return pl.pallas_call(
        functools.partial(_expert_up_kernel, n_in=len(xs_list)),
        grid=(e, f // tf),
        in_specs=[pl.BlockSpec((1, x.shape[1], d), lambda i, j: (i, 0, 0)) for x in xs_list] + [wblk, wblk],
        out_specs=[pl.BlockSpec((1, x.shape[1], tf), lambda i, j: (i, 0, j)) for x in xs_list],
        out_shape=[jax.ShapeDtypeStruct((e, x.shape[1], f), BF16) for x in xs_list],
        compiler_params=_cp(("arbitrary", "arbitrary"), 56),
        name="expert_up",
    )(*xs_list, w_gate_all, w_up_all)


def _expert_down_kernel(*refs, n_in):
    h_refs, g_refs, w_ref, o_refs = refs[:n_in], refs[n_in:2 * n_in], refs[2 * n_in], refs[2 * n_in + 1:]
    w = w_ref[0, 0].astype(BF16)
    for h_ref, g_ref, o_ref in zip(h_refs, g_refs, o_refs):
        o_ref[0] = (_dot(h_ref[0], w) * g_ref[0]).astype(BF16)


def _expert_down(hid_list, gs_list, w_down_all, layer):
    e, _, f = hid_list[0].shape
    d = w_down_all.shape[-1]
    tn = _tile(d, 512)
    return pl.pallas_call(
        functools.partial(_expert_down_kernel, n_in=len(hid_list)),
        grid=(e, d // tn),
        in_specs=[pl.BlockSpec((1, h.shape[1], f), lambda i, j: (i, 0, 0)) for h in hid_list]
        + [pl.BlockSpec((1, g.shape[1], 1), lambda i, j: (i, 0, 0)) for g in gs_list]
        + [pl.BlockSpec((1, 1, f, tn), lambda i, j: (layer, i, 0, j))],
        out_specs=[pl.BlockSpec((1, h.shape[1], tn), lambda i, j: (i, 0, j)) for h in hid_list],
        out_shape=[jax.ShapeDtypeStruct((e, h.shape[1], d), BF16) for h in hid_list],
        compiler_params=_cp(("arbitrary", "arbitrary"), 56),
        name="expert_down",
    )(*hid_list, *gs_list, w_down_all)


def _combine_kernel(cum_ref, pos_ref, x_ref, mod_ref, gain_ref, modn_ref, ys_hbm, *rest,
                    n_exp, capp, win, chunks_per_step, final):
    n_out = 1 if final else 2
    outs, (ybuf, xbuf, acc_ref, sem, xsem) = rest[:n_out], rest[n_out:]
    b, t = pl.program_id(0), pl.program_id(1)
    n_t = pl.num_programs(1)
    step = b * n_t + t
    n_steps = pl.num_programs(0) * n_t
    slot = lax.rem(step, 2)
    tokens = pos_ref.shape[2]
    slot_iota = lax.broadcasted_iota(I32, (win, tokens), 0)

    def win_start(bb, tt, e):
        lo = cum_ref[(bb * n_exp + e) * LANE + tt * chunks_per_step]
        return jnp.minimum((lo // BF16_ROWS) * BF16_ROWS, capp - win)

    def window_copy(bb, tt, e, s):
        src0 = pl.multiple_of(bb * capp + win_start(bb, tt, e), BF16_ROWS)
        return pltpu.make_async_copy(ys_hbm.at[e, pl.ds(src0, win), :],
                                     ybuf.at[s, pl.ds(e * win, win), :], sem.at[s])

    def start_windows(bb, tt, s):
        for e in range(n_exp):
            window_copy(bb, tt, e, s).start()

    @pl.when(step == 0)
    def _():
        start_windows(b, t, slot)

    @pl.when(step + 1 < n_steps)
    def _():
        nxt = step + 1
        start_windows(nxt // n_t, lax.rem(nxt, n_t), 1 - slot)

    onehot = jnp.concatenate(
        [_onehot(pos_ref[0, e:e + 1, :] == slot_iota + win_start(b, t, e)) for e in range(n_exp)], axis=0)
    for e in range(n_exp):
        window_copy(b, t, e, slot).wait()
    acc_ref[...] = _dot_tn(onehot, ybuf[slot])

    def long_run(e, carry):
        ws = win_start(b, t, e)
        hi = cum_ref[(b * n_exp + e) * LANE + (t + 1) * chunks_per_step]

        def extra(j, c):
            first = ws + j * win
            wj = jnp.minimum(first, capp - win)
            src0 = pl.multiple_of(b * capp + wj, BF16_ROWS)
            cp = pltpu.make_async_copy(ys_hbm.at[e, pl.ds(src0, win), :], xbuf, xsem)
            cp.start()
            cp.wait()
            sl = slot_iota + wj
            hit = (pos_ref[0, pl.ds(e, 1), :] == sl) & (sl >= first)
            acc_ref[...] += _dot_tn(_onehot(hit), xbuf[...])
            return c

        lax.fori_loop(1, (hi - ws + win - 1) // win, extra, 0)
        return carry

    longest = functools.reduce(jnp.maximum, [
        cum_ref[(b * n_exp + e) * LANE + (t + 1) * chunks_per_step] - win_start(b, t, e) for e in range(n_exp)])

    @pl.when(longest > win)
    def _():
        lax.fori_loop(0, n_exp, long_run, 0)

    y = x_ref[0] + mod_ref[0, 0][5:6] * acc_ref[...]
    if final:
        outs[0][0] = _rms(y, gain_ref[0])
    else:
        outs[0][0] = y
        outs[1][0] = _modulated_norm(y, gain_ref[0], modn_ref[0, 0], 0, 1).astype(BF16)


def _combine(ys, pos, cum, x1, mod_all, layer, row_of, capp, gain_all, gain_layer, final):
    b, n, d = x1.shape
    e = pos.shape[1]
    tokens = min(COMBINE_TOKENS, n)
    win = min(COMBINE_WIN, capp)
    assert n % tokens == 0 and tokens % LANE == 0 and capp % BF16_ROWS == 0 and win % BF16_ROWS == 0
    mod_next_layer = layer if final else layer + 1
    row = pl.BlockSpec((1, tokens, d), lambda i, t, *_: (i, t, 0))
    grid_spec = pltpu.PrefetchScalarGridSpec(
        num_scalar_prefetch=1,
        grid=(b, n // tokens),
        in_specs=[
            pl.BlockSpec((1, e, tokens), lambda i, t, *_: (i, 0, t)),
            row,
            _mod_spec(mod_all, layer, row_of),
            _gain_spec(gain_all, gain_layer),
            _mod_spec(mod_all, mod_next_layer, row_of),
            pl.BlockSpec(memory_space=pl.ANY),
        ],
        out_specs=[row] if final else [row, row],
        scratch_shapes=[pltpu.VMEM((2, e * win, d), BF16), pltpu.VMEM((win, d), BF16),
                        pltpu.VMEM((tokens, d), F32),
                        pltpu.SemaphoreType.DMA((2,)), pltpu.SemaphoreType.DMA],
    )
    out_f32 = jax.ShapeDtypeStruct((b, n, d), F32)
    return pl.pallas_call(
        functools.partial(_combine_kernel, n_exp=e, capp=capp, win=win,
                          chunks_per_step=tokens // LANE, final=final),
        grid_spec=grid_spec,
        out_shape=[out_f32] if final else [out_f32, jax.ShapeDtypeStruct((b, n, d), BF16)],
        compiler_params=_cp(("arbitrary", "arbitrary")),
        name="moe_combine",
    )(cum.reshape(-1), pos, x1, mod_all, gain_all, mod_all, ys)


def _route_and_gather(h2, aff):
    n, e = h2.shape[1], aff.shape[1]
    cap = max(1, CAPACITY_FACTOR * n // e)
    capp = -(-cap // BF16_ROWS) * BF16_ROWS
    pos, cum = _route(aff, cap)
    xs, gs = _gather(h2, pos, aff, cum, capp)
    return dict(xs=xs, gs=gs, pos=pos, cum=cum, capp=capp)


def kernel(x, c, ctx, c_ctx, w_mod, b_mod, norm1, norm2, w_in, conv_w, pool_w, pool_scale, rpb,
           w_conv_out, w_pool_out, w_attn_out, w_o, w_router, w_e_gate, w_e_up, w_e_down, final_norm):
    depth = w_mod.shape[0]
    b, n, d = x.shape
    dc, dp = conv_w.shape[-1], pool_scale.shape[-1]
    n_heads, da = rpb.shape[1], w_attn_out.shape[1]
    off_pool, off_q = 3 * dc, 3 * dc + dp
    off_k, off_gate = off_q + da, off_q + 3 * da
    assert n % GRID_W == 0 and da % n_heads == 0

    n_rows = -(-(b + 1) // SUBLANE) * SUBLANE
    cond = jnp.zeros((n_rows, d), F32).at[:b].set(c).at[b].set(c_ctx)
    mod_all = _adaln_all(cond, w_mod, b_mod).reshape(depth, n_rows, N_MOD, d)
    lat_row, ctx_row = (lambda i: i), (lambda i: b)
    bias_all = _bias_tiles(rpb, n // GRID_W)

    w_in_b = w_in.astype(BF16)
    wc_b, wp_b, wa_b, wo_b = (w.astype(BF16) for w in (w_conv_out, w_pool_out, w_attn_out, w_o))
    wr_t = jnp.swapaxes(w_router, 1, 2).astype(BF16)
    pool_w_b = pool_w.astype(BF16)
    pool_scale_r = pool_scale.reshape(depth, pool_w.shape[1], 1, -1)
    gain1, gain2 = norm1.reshape(depth, 1, d), norm2.reshape(depth, 1, d)
    gain_f = final_norm.reshape(1, 1, d)

    def mixer(h, attn, l, as_rows):
        zc = _conv_branch(h, w_in_b, conv_w, l, dc)
        zp = _pool_branch(h, w_in_b, off_pool, pool_w_b, pool_scale_r, l)
        return _merge(as_rows(h), as_rows(zc), as_rows(zp), as_rows(attn), w_in_b, off_gate, wc_b, wp_b, wa_b, l)

    joined = lambda a: a.reshape(1, -1, a.shape[-1])
    per_sample = lambda a: a.reshape(b, -1, a.shape[-1])
    keep = lambda a: a

    hx = _norm_mod(x, gain1, mod_all, 0, lat_row)
    hc = _norm_mod(ctx, gain1, mod_all, 0, ctx_row)
    for l in range(depth):
        last = l == depth - 1
        groups = []
        if last:
            kv_c = per_sample(_proj(joined(hc), w_in_b, l, off_k, 2 * da, BF16))
            k_off, v_off = 0, n_heads
        else:
            kv_c = per_sample(_proj(joined(hc), w_in_b, l, off_q, 3 * da, BF16))
            k_off, v_off = n_heads, 2 * n_heads
            merged_c = mixer(hc, _ctx_attention(kv_c, n_heads), l, joined)
            ctx1, hc2, aff_c = _out_proj(merged_c, wo_b, joined(ctx), mod_all, gain2, wr_t, l, ctx_row)
            ctx1, hc2 = per_sample(ctx1), per_sample(hc2)
            aff_c = jnp.swapaxes(aff_c.reshape(aff_c.shape[1], b, -1), 0, 1)
            groups.append(dict(_route_and_gather(hc2, aff_c), x1=ctx1, row_of=ctx_row))

        qkv = _proj(hx, w_in_b, l, off_q, 3 * da, BF16)
        attn = _natten(qkv, kv_c, k_off, v_off, bias_all, l, n_heads)
        x1, hx2, aff_x = _out_proj(mixer(hx, attn, l, keep), wo_b, x, mod_all, gain2, wr_t, l, lat_row)
        groups.append(dict(_route_and_gather(hx2, aff_x), x1=x1, row_of=lat_row))

        hid = _expert_up([g["xs"] for g in groups], w_e_gate, w_e_up, l)
        ys = _expert_down(hid, [g["gs"] for g in groups], w_e_down, l)
        res = [_combine(y, g["pos"], g["cum"], g["x1"], mod_all, l, g["row_of"], g["capp"],
                        gain_f if last else gain1, 0 if last else l + 1, last)
               for y, g in zip(ys, groups)]
        if last:
            return res[-1][0]
        (ctx, hc), (x, hx) = res
```
